```python
import math
import jax, jax.numpy as jnp
from jax import lax
import numpy as np

D_MODEL = 4096
BATCH = 4
SEQ = 2048
DEPTH = 4
DEC_BATCH = 128
DEC_SEQ = 1
PAST_LEN = 8192
PAGE_SIZE = 128

NORM_EPS = 1e-6
D_FF = -(-8 * D_MODEL // (3 * 256)) * 256
MLA_HEADS = 16
Q_LORA = 896
KV_LORA = 256
NOPE_DIM = 128
ROPE_DIM = 64
MLA_V_DIM = 128
ROPE_BASE = 10000.0
MLA_ROW = KV_LORA + ROPE_DIM
DIFF_HEADS = 8
DIFF_D = 128
ATTN_SPLIT = (Q_LORA, KV_LORA, ROPE_DIM, DIFF_HEADS * 2 * DIFF_D, 2 * DIFF_D, 2 * DIFF_D)
ATTN_IN = sum(ATTN_SPLIT)
ATTN_OUT = MLA_HEADS * MLA_V_DIM + DIFF_HEADS * 2 * DIFF_D
QUERY_BLOCK = 128
D_INNER = 2 * D_MODEL
SSM_HEAD_DIM = 64
SSM_HEADS = D_INNER // SSM_HEAD_DIM
SSM_GROUPS = 8
SSM_STATE = 128
HEADS_PER_GROUP = SSM_HEADS // SSM_GROUPS
CONV_W = 4
CONV_DIM = D_INNER + 2 * SSM_GROUPS * SSM_STATE
SSM_IN = D_INNER + CONV_DIM + SSM_HEADS
SSD_CHUNK = 256
N_ATTN_LAYERS = DEPTH // 2
N_SSM_LAYERS = DEPTH // 2

kernel_name = 'hybrid_mla_diffattn_ssd_decode_step'

F32 = jnp.float32


def _rms(x, g):
    xf = x.astype(F32)
    y = xf * lax.rsqrt(jnp.mean(xf * xf, axis=-1, keepdims=True) + NORM_EPS)
    return (y * g.astype(F32)).astype(x.dtype)


def _group_rms(x, g, groups):
    shp = x.shape
    xf = x.astype(F32).reshape(shp[:-1] + (groups, shp[-1] // groups))
    y = xf * lax.rsqrt(jnp.mean(xf * xf, axis=-1, keepdims=True) + NORM_EPS)
    return y.reshape(shp) * g.astype(F32)


def _split(u, sizes):
    out, start = [], 0
    for s in sizes:
        out.append(u[..., start:start + s])
        start += s
    return out


def _rope(x, pos):
    half = ROPE_DIM // 2
    inv = jnp.exp(-math.log(ROPE_BASE) * jnp.arange(half, dtype=F32) / half)
    ang = pos.astype(F32)[:, None] * inv[None, :]
    shape = (1, pos.shape[0]) + (1,) * (x.ndim - 3) + (half,)
    cos, sin = jnp.cos(ang).reshape(shape), jnp.sin(ang).reshape(shape)
    xf = x.astype(F32)
    x1, x2 = xf[..., :half], xf[..., half:]
    return jnp.concatenate([x1 * cos - x2 * sin, x1 * sin + x2 * cos], axis=-1).astype(x.dtype)


def _sweep_queries(fn, q, q_pos):
    n, tq = q.shape[:2]
    qb = math.gcd(tq, QUERY_BLOCK)
    nb = tq // qb
    qs = jnp.moveaxis(q.reshape((n, nb, qb) + q.shape[2:]), 1, 0)
    out = lax.map(lambda a: fn(a[0], a[1]), (qs, q_pos.reshape(nb, qb)))
    return jnp.moveaxis(out, 0, 1).reshape((n, tq) + out.shape[3:])


def _mla_attention(q, keys, q_pos):
    k_pos = jnp.arange(keys.shape[1])
    kf = keys.astype(F32)
    vf = kf[..., :KV_LORA]
    scale = (NOPE_DIM + ROPE_DIM) ** -0.5

    def block(qb, pb):
        s = jnp.einsum('nqhc,nkc->nhqk', qb.astype(F32), kf) * scale
        mask = k_pos[None, :] <= pb[:, None]
        p = jax.nn.softmax(jnp.where(mask, s, -jnp.inf), axis=-1)
        return jnp.einsum('nhqk,nkc->nqhc', p, vf)

    return _sweep_queries(block, q, q_pos)


def _diff_attention(q, k, v, q_pos, lam):
    k_pos = jnp.arange(k.shape[1])
    kf, vf = k.astype(F32), v.astype(F32)
    scale = DIFF_D ** -0.5

    def block(qb, pb):
        s = jnp.einsum('nqhid,nkid->nihqk', qb.astype(F32), kf) * scale
        mask = k_pos[None, :] <= pb[:, None]
        p = jax.nn.softmax(jnp.where(mask, s, -jnp.inf), axis=-1)
        a = p[:, 0] - lam * p[:, 1]
        return jnp.einsum('nhqk,nkv->nqhv', a, vf)

    return _sweep_queries(block, q, q_pos)


def _attn_mixer(h, pos, past, lam_init, w_in, g_q, g_kv, w_qb, w_kvb, lam_p, g_sub, w_out):
    n, t, _ = h.shape
    cq, ckv, kpe, dq, dk, dv = _split(h @ w_in, ATTN_SPLIT)
    q = jnp.einsum('ntc,chd->nthd', _rms(cq, g_q), w_qb)
    q_nope, q_pe = q[..., :NOPE_DIM], _rope(q[..., NOPE_DIM:], pos)
    mla_row = jnp.concatenate([_rms(ckv, g_kv), _rope(kpe, pos)], axis=-1)
    q_lat = jnp.einsum('nthd,chd->nthc', q_nope, w_kvb[..., :NOPE_DIM])
    q_mla = jnp.concatenate([q_lat, q_pe.astype(q_lat.dtype)], axis=-1)
    dq = dq.reshape(n, t, DIFF_HEADS, 2, DIFF_D)
    dk = dk.reshape(n, t, 2, DIFF_D)
    if past is None:
        keys_mla, keys_d, vals_d = mla_row, dk, dv
    else:
        keys_mla = jnp.concatenate([past[0].astype(mla_row.dtype), mla_row], axis=1)
        keys_d = jnp.concatenate([past[1].astype(dk.dtype), dk], axis=1)
        vals_d = jnp.concatenate([past[2].astype(dv.dtype), dv], axis=1)
    o_lat = _mla_attention(q_mla, keys_mla, pos)
    o_mla = jnp.einsum('nthc,chv->nthv', o_lat, w_kvb[..., NOPE_DIM:].astype(F32))
    lp = lam_p.astype(F32)
    lam = jnp.exp(jnp.sum(lp[0] * lp[1])) - jnp.exp(jnp.sum(lp[2] * lp[3])) + lam_init
    o_d = _diff_attention(dq, keys_d, vals_d, pos, lam)
    o_d = _rms(o_d, g_sub) * (1.0 - lam_init)
    o = jnp.concatenate([o_mla.reshape(n, t, -1), o_d.reshape(n, t, -1)], axis=-1).astype(h.dtype)
    return o @ w_out, (mla_row, dk, dv)


def _ssd_scan(x, dt, a, b, c, h0):
    n, t = x.shape[:2]
    L = math.gcd(t, SSD_CHUNK)
    nc = t // L
    chunk = lambda arr: jnp.moveaxis(arr.reshape((n, nc, L) + arr.shape[2:]), 1, 0)
    causal = jnp.tril(jnp.ones((L, L), bool))[None, :, :, None, None]

    def step(h, inp):
        xc, dtc, bc, cc = inp
        acum = jnp.cumsum(dtc * a, axis=1)
        seg = acum[:, :, None] - acum[:, None, :]
        decay = jnp.exp(jnp.where(causal, seg, -jnp.inf))
        cb = jnp.einsum('nigs,njgs->nijg', cc, bc)
        wgt = cb[..., None] * decay * dtc[:, None]
        y = (jnp.einsum('nijgr,njgrp->nigrp', wgt, xc)
             + jnp.einsum('nigs,ngrps->nigrp', cc, h) * jnp.exp(acum)[..., None])
        to_end = jnp.exp(acum[:, -1:] - acum) * dtc
        h_new = (h * jnp.exp(acum[:, -1])[..., None, None]
                 + jnp.einsum('njgs,njgr,njgrp->ngrps', bc, to_end, xc))
        return h_new, y

    h_fin, ys = lax.scan(step, h0, (chunk(x), chunk(dt), chunk(b), chunk(c)))
    y = jnp.moveaxis(ys, 0, 1).reshape(x.shape)
    return y, h_fin


def _ssd_mixer(h, conv_state, ssm_state, w_in, conv_w, conv_b, dt_bias, a_log, d_skip, g_norm, w_out):
    n, t, _ = h.shape
    z, xbc, dt = _split(h @ w_in, (D_INNER, CONV_DIM, SSM_HEADS))
    xpad = jnp.concatenate([conv_state.astype(xbc.dtype), xbc], axis=1)
    conv = conv_b + xpad[:, 0:t] * conv_w[0]
    for k in range(1, CONV_W):
        conv = conv + xpad[:, k:k + t] * conv_w[k]
    xbc = jax.nn.silu(conv)
    xs, b, c = _split(xbc, (D_INNER, SSM_GROUPS * SSM_STATE, SSM_GROUPS * SSM_STATE))
    xs = xs.astype(F32).reshape(n, t, SSM_GROUPS, HEADS_PER_GROUP, SSM_HEAD_DIM)
    b = b.astype(F32).reshape(n, t, SSM_GROUPS, SSM_STATE)
    c = c.astype(F32).reshape(n, t, SSM_GROUPS, SSM_STATE)
    dt = jax.nn.softplus(dt.astype(F32) + dt_bias.astype(F32)).reshape(n, t, SSM_GROUPS, HEADS_PER_GROUP)
    a = -jnp.exp(a_log.astype(F32)).reshape(SSM_GROUPS, HEADS_PER_GROUP)
    h0 = ssm_state.astype(F32).reshape(n, SSM_GROUPS, HEADS_PER_GROUP, SSM_HEAD_DIM, SSM_STATE)
    y, h_fin = _ssd_scan(xs, dt, a, b, c, h0)
    y = y + d_skip.astype(F32).reshape(SSM_GROUPS, HEADS_PER_GROUP)[..., None] * xs
    y = y.reshape(n, t, D_INNER) * jax.nn.silu(z.astype(F32))
    y = _group_rms(y, g_norm, SSM_GROUPS).astype(h.dtype)
    new_ssm = h_fin.reshape(n, SSM_HEADS, SSM_HEAD_DIM, SSM_STATE).astype(ssm_state.dtype)
    return y @ w_out, xpad[:, t:], new_ssm


def _swiglu(h, wg, wu, wd):
    return (jax.nn.silu(h @ wg) * (h @ wu)) @ wd


def _gather_pages(pool, page_table):
    rows = jnp.take(pool, page_table, axis=0)
    return rows.reshape((page_table.shape[0], page_table.shape[1] * pool.shape[1]) + pool.shape[2:])


def setup_inputs(seed: int = 0) -> dict:
    key = jax.random.key(seed)
    ks = iter(jax.random.split(key, 64))
    nrm = lambda shape, scale: jax.random.normal(next(ks), shape, F32) * scale
    gain = lambda shape: 1.0 + nrm(shape, 0.05)
    n_pages = PAST_LEN // PAGE_SIZE
    n_used = DEC_BATCH * n_pages
    n_pool = n_used + n_used // 4
    page_table = jax.random.permutation(next(ks), n_pool)[:n_used].reshape(DEC_BATCH, n_pages).astype(jnp.int32)
    NA, NS = N_ATTN_LAYERS, N_SSM_LAYERS
    dt0 = jnp.exp(jax.random.uniform(next(ks), (NS, SSM_HEADS), F32, math.log(1e-3), math.log(1e-1)))
    ssm_dt_bias = dt0 + jnp.log(-jnp.expm1(-dt0))
    ssm_a_log = jnp.log(jax.random.uniform(next(ks), (NS, SSM_HEADS), F32, 1.0, 16.0))
    return {
        'x_prompt': nrm((BATCH, SEQ, D_MODEL), 1.0),
        'x_sample': nrm((DEC_BATCH, DEC_SEQ, D_MODEL), 1.0),
        'cache_mla_l0': nrm((n_pool, PAGE_SIZE, MLA_ROW), 1.0),
        'cache_dk_l0': nrm((n_pool, PAGE_SIZE, 2, DIFF_D), 1.0),
        'cache_dv_l0': nrm((n_pool, PAGE_SIZE, 2 * DIFF_D), 1.0),
        'state_conv_l1': nrm((DEC_BATCH, CONV_W - 1, CONV_DIM), 1.0),
        'state_ssm_l1': nrm((DEC_BATCH, SSM_HEADS, SSM_HEAD_DIM, SSM_STATE), 0.1),
        'cache_mla_l2': nrm((n_pool, PAGE_SIZE, MLA_ROW), 1.0),
        'cache_dk_l2': nrm((n_pool, PAGE_SIZE, 2, DIFF_D), 1.0),
        'cache_dv_l2': nrm((n_pool, PAGE_SIZE, 2 * DIFF_D), 1.0),
        'state_conv_l3': nrm((DEC_BATCH, CONV_W - 1, CONV_DIM), 1.0),
        'state_ssm_l3': nrm((DEC_BATCH, SSM_HEADS, SSM_HEAD_DIM, SSM_STATE), 0.1),
        'page_table': page_table,
        'norm_mix_pre': gain((DEPTH, D_MODEL)),
        'norm_mix_post': gain((DEPTH, D_MODEL)),
        'norm_ffn_pre': gain((DEPTH, D_MODEL)),
        'norm_ffn_post': gain((DEPTH, D_MODEL)),
        'ffn_w_gate': nrm((DEPTH, D_MODEL, D_FF), D_MODEL ** -0.5),
        'ffn_w_up': nrm((DEPTH, D_MODEL, D_FF), D_MODEL ** -0.5),
        'ffn_w_down': nrm((DEPTH, D_FF, D_MODEL), D_FF ** -0.5),
        'attn_w_in': nrm((NA, D_MODEL, ATTN_IN), D_MODEL ** -0.5),
        'mla_g_q': gain((NA, Q_LORA)),
        'mla_g_kv': gain((NA, KV_LORA)),
        'mla_w_qb': nrm((NA, Q_LORA, MLA_HEADS, NOPE_DIM + ROPE_DIM), Q_LORA ** -0.5),
        'mla_w_kvb': nrm((NA, KV_LORA, MLA_HEADS, NOPE_DIM + MLA_V_DIM), KV_LORA ** -0.5),
        'diff_lambda': nrm((NA, 4, DIFF_D), 0.1),
        'diff_g_sub': gain((NA, 2 * DIFF_D)),
        'attn_w_out': nrm((NA, ATTN_OUT, D_MODEL), ATTN_OUT ** -0.5),
        'ssm_w_in': nrm((NS, D_MODEL, SSM_IN), D_MODEL ** -0.5),
        'ssm_conv_w': nrm((NS, CONV_W, CONV_DIM), CONV_W ** -0.5),
        'ssm_conv_b': nrm((NS, CONV_DIM), 0.01),
        'ssm_dt_bias': ssm_dt_bias,
        'ssm_a_log': ssm_a_log,
        'ssm_d': 1.0 + nrm((NS, SSM_HEADS), 0.1),
        'ssm_g_norm': gain((NS, D_INNER)),
        'ssm_w_out': nrm((NS, D_INNER, D_MODEL), D_INNER ** -0.5),
    }


def reference(x_prompt, x_sample, cache_mla_l0, cache_dk_l0, cache_dv_l0, state_conv_l1, state_ssm_l1,
              cache_mla_l2, cache_dk_l2, cache_dv_l2, state_conv_l3, state_ssm_l3, page_table,
              norm_mix_pre, norm_mix_post, norm_ffn_pre, norm_ffn_post, ffn_w_gate, ffn_w_up, ffn_w_down,
              attn_w_in, mla_g_q, mla_g_kv, mla_w_qb, mla_w_kvb, diff_lambda, diff_g_sub, attn_w_out,
              ssm_w_in, ssm_conv_w, ssm_conv_b, ssm_dt_bias, ssm_a_log, ssm_d, ssm_g_norm, ssm_w_out):
    past_len = page_table.shape[1] * PAGE_SIZE
    pos_p = jnp.arange(x_prompt.shape[1])
    pos_s = past_len + jnp.arange(x_sample.shape[1])
    attn_caches = {0: (cache_mla_l0, cache_dk_l0, cache_dv_l0), 2: (cache_mla_l2, cache_dk_l2, cache_dv_l2)}
    ssm_states = {1: (state_conv_l1, state_ssm_l1), 3: (state_conv_l3, state_ssm_l3)}
    new = {}
    xp, xs = x_prompt, x_sample
    for layer in range(DEPTH):
        hp = _rms(xp, norm_mix_pre[layer])
        hs = _rms(xs, norm_mix_pre[layer])
        j = layer // 2
        if layer % 2 == 0:
            lam_init = 0.8 - 0.6 * math.exp(-0.3 * layer)
            wts = (attn_w_in[j], mla_g_q[j], mla_g_kv[j], mla_w_qb[j], mla_w_kvb[j],
                   diff_lambda[j], diff_g_sub[j], attn_w_out[j])
            op, rows_p = _attn_mixer(hp, pos_p, None, lam_init, *wts)
            past = tuple(_gather_pages(cc, page_table) for cc in attn_caches[layer])
            os_, rows_s = _attn_mixer(hs, pos_s, past, lam_init, *wts)
            new[layer] = rows_p + rows_s
        else:
            wts = (ssm_w_in[j], ssm_conv_w[j], ssm_conv_b[j], ssm_dt_bias[j], ssm_a_log[j],
                   ssm_d[j], ssm_g_norm[j], ssm_w_out[j])
            conv0 = jnp.zeros((xp.shape[0], CONV_W - 1, CONV_DIM), xp.dtype)
            ssm0 = jnp.zeros((xp.shape[0], SSM_HEADS, SSM_HEAD_DIM, SSM_STATE), xp.dtype)
            op, conv_p, ssm_p = _ssd_mixer(hp, conv0, ssm0, *wts)
            conv_c, ssm_c = ssm_states[layer]
            os_, conv_s, ssm_s = _ssd_mixer(hs, conv_c, ssm_c, *wts)
            new[layer] = (conv_p, ssm_p, conv_s, ssm_s)
        xp = xp + _rms(op, norm_mix_post[layer])
        xs = xs + _rms(os_, norm_mix_post[layer])
        xp = xp + _rms(_swiglu(_rms(xp, norm_ffn_pre[layer]), ffn_w_gate[layer], ffn_w_up[layer], ffn_w_down[layer]), norm_ffn_post[layer])
        xs = xs + _rms(_swiglu(_rms(xs, norm_ffn_pre[layer]), ffn_w_gate[layer], ffn_w_up[layer], ffn_w_down[layer]), norm_ffn_post[layer])
    return (xp, xs,
            new[0][0], new[0][1], new[0][2], new[0][3], new[0][4], new[0][5],
            new[1][0], new[1][1], new[1][2], new[1][3],
            new[2][0], new[2][1], new[2][2], new[2][3], new[2][4], new[2][5],
            new[3][0], new[3][1], new[3][2], new[3][3])
```

```python
import functools
import math

import jax
import jax.numpy as jnp
from jax import lax
from jax.experimental import pallas as pl
from jax.experimental.pallas import tpu as pltpu

F32 = jnp.float32
BF16 = jnp.bfloat16
NORM_EPS = 1e-6
ROPE_BASE = 10000.0
NEG = -1e30
LANES = 128
VMEM_LIMIT = 56 * 1024 * 1024
FFN_ALIGN = 1024


def _pick(n, target, align):
    if n <= target:
        return n
    d = target - target % align
    while d >= align:
        if n % d == 0:
            return d
        d -= align
    return n


def _params(*sem):
    return pltpu.CompilerParams(dimension_semantics=sem, vmem_limit_bytes=VMEM_LIMIT)


def _rms_rows(x, g):
    return x * lax.rsqrt(jnp.mean(x * x, axis=-1, keepdims=True) + NORM_EPS) * g


def _silu(x):
    return x * jax.nn.sigmoid(x)


def _dot(a, b):
    return jnp.dot(a, b, preferred_element_type=F32)


def _dot_nt(a, b):
    return lax.dot_general(a, b, (((1,), (1,)), ((), ())), preferred_element_type=F32)


def _dot_tn(a, b):
    return lax.dot_general(a, b, (((0,), (0,)), ((), ())), preferred_element_type=F32)


def _rms_kernel(x_ref, g_ref, o_ref):
    o_ref[...] = _rms_rows(x_ref[...], g_ref[...]).astype(o_ref.dtype)


def _rms(x, g, out_dtype=BF16):
    m, d = x.shape
    tm = _pick(m, 512, 16)
    return pl.pallas_call(
        _rms_kernel,
        grid=(m // tm,),
        in_specs=[pl.BlockSpec((tm, d), lambda i: (i, 0)), pl.BlockSpec((1, d), lambda i: (0, 0))],
        out_specs=pl.BlockSpec((tm, d), lambda i: (i, 0)),
        out_shape=jax.ShapeDtypeStruct((m, d), out_dtype),
        compiler_params=_params("parallel"),
        name="rms",
    )(x, g.reshape(1, d))


def _add_rms_kernel(x_ref, o_ref, gpost_ref, *rest, with_next):
    xn = x_ref[...] + _rms_rows(o_ref[...], gpost_ref[...])
    if with_next:
        gnext_ref, xo_ref, h_ref = rest
        h_ref[...] = _rms_rows(xn, gnext_ref[...]).astype(h_ref.dtype)
    else:
        (xo_ref,) = rest
    xo_ref[...] = xn


def _add_rms(x, o, g_post, g_next=None):
    m, d = x.shape
    tm = _pick(m, 256, 16)
    row = pl.BlockSpec((tm, d), lambda i: (i, 0))
    vec = pl.BlockSpec((1, d), lambda i: (0, 0))
    with_next = g_next is not None
    in_specs = [row, row, vec] + ([vec] if with_next else [])
    args = [x, o, g_post.reshape(1, d)] + ([g_next.reshape(1, d)] if with_next else [])
    out_shape = [jax.ShapeDtypeStruct((m, d), F32)] + ([jax.ShapeDtypeStruct((m, d), BF16)] if with_next else [])
    out = pl.pallas_call(
        functools.partial(_add_rms_kernel, with_next=with_next),
        grid=(m // tm,),
        in_specs=in_specs,
        out_specs=[row] * len(out_shape),
        out_shape=out_shape,
        compiler_params=_params("parallel"),
        name="add_rms",
    )(*args)
    return (out[0], out[1]) if with_next else (out[0], None)


def _mm_kernel(x_ref, w_ref, o_ref, *scratch, nk, scale):
    def finish(acc):
        return (acc * scale if scale != 1.0 else acc).astype(o_ref.dtype)

    if nk == 1:
        o_ref[...] = finish(_dot(x_ref[...], w_ref[...]))
        return
    acc_ref = scratch[0] if scratch else o_ref
    k = pl.program_id(2)

    @pl.when(k == 0)
    def _():
        acc_ref[...] = jnp.zeros_like(acc_ref)

    acc_ref[...] += _dot(x_ref[...], w_ref[...])
    if scratch or scale != 1.0:
        @pl.when(k == nk - 1)
        def _():
            o_ref[...] = finish(acc_ref[...])


def _matmul(x, w, out_dtype=F32, scale=1.0, bm=1040, bn=1024, bk=4096):
    m, kdim = x.shape
    n = w.shape[1]
    bm = _pick(m, bm, 16)
    bn = _pick(n, bn, LANES)
    bk = _pick(kdim, bk, LANES)
    nk = kdim // bk
    use_scratch = nk > 1 and out_dtype != F32
    return pl.pallas_call(
        functools.partial(_mm_kernel, nk=nk, scale=scale),
        grid=(m // bm, n // bn, nk),
        in_specs=[pl.BlockSpec((bm, bk), lambda i, j, k: (i, k)), pl.BlockSpec((bk, bn), lambda i, j, k: (k, j))],
        out_specs=pl.BlockSpec((bm, bn), lambda i, j, k: (i, j)),
        out_shape=jax.ShapeDtypeStruct((m, n), out_dtype),
        scratch_shapes=[pltpu.VMEM((bm, bn), F32)] if use_scratch else [],
        compiler_params=_params("parallel", "parallel", "arbitrary"),
        name="matmul",
    )(x, w)


def _swiglu_kernel(x_ref, wg_ref, wu_ref, o_ref):
    x = x_ref[...]
    o_ref[...] = (_silu(_dot(x, wg_ref[...])) * _dot(x, wu_ref[...])).astype(o_ref.dtype)


def _swiglu_up(x, wg, wu):
    m, kdim = x.shape
    n = wg.shape[1]
    bm = _pick(m, 1040, 16)
    bn = _pick(n, 512, LANES)
    wspec = pl.BlockSpec((kdim, bn), lambda i, j: (0, j))
    return pl.pallas_call(
        _swiglu_kernel,
        grid=(m // bm, n // bn),
        in_specs=[pl.BlockSpec((bm, kdim), lambda i, j: (i, 0)), wspec, wspec],
        out_specs=pl.BlockSpec((bm, bn), lambda i, j: (i, j)),
        out_shape=jax.ShapeDtypeStruct((m, n), BF16),
        compiler_params=_params("parallel", "parallel"),
        name="swiglu_up",
    )(x, wg, wu)


def _rope_pair(pe, cos, sin):
    return pe * cos + pltpu.roll(pe, LANES // 2, 1) * sin


def _qprep_kernel(cq_ref, g_ref, wq_ref, wk_ref, cos_ref, sin_ref, o_ref, *, nheads, nope, dlat, scale):
    cqn = _rms_rows(cq_ref[...], g_ref[...]).astype(BF16)
    q = _dot(cqn, wq_ref[...])
    cos, sin = cos_ref[...], sin_ref[...]
    pe0 = nheads * nope
    for h in range(nheads):
        qn = q[:, h * nope:(h + 1) * nope].astype(BF16)
        o_ref[h, :, 0:dlat] = (_dot(qn, wk_ref[h]) * scale).astype(o_ref.dtype)
        pe = q[:, pe0 + h * LANES:pe0 + (h + 1) * LANES]
        o_ref[h, :, dlat:dlat + LANES] = (_rope_pair(pe, cos, sin) * scale).astype(o_ref.dtype)


def _qprep(cq, g_q, wq, wk, cos, sin, scale):
    m, cdim = cq.shape
    nheads, nope, dlat = wk.shape
    tm = _pick(m, 256, 16)
    full = lambda a: pl.BlockSpec(a.shape, lambda i: (0,) * a.ndim)
    row = lambda a: pl.BlockSpec((tm, a.shape[1]), lambda i: (i, 0))
    return pl.pallas_call(
        functools.partial(_qprep_kernel, nheads=nheads, nope=nope, dlat=dlat, scale=scale),
        grid=(m // tm,),
        in_specs=[row(cq), full(g_q), full(wq), full(wk), row(cos), row(sin)],
        out_specs=pl.BlockSpec((nheads, tm, dlat + LANES), lambda i: (0, i, 0)),
        out_shape=jax.ShapeDtypeStruct((nheads, m, dlat + LANES), BF16),
        compiler_params=_params("parallel"),
        name="mla_qprep",
    )(cq, g_q, wq, wk, cos, sin)


def _kvprep_kernel(u_ref, g_ref, cos_ref, sin_ref, row_ref, k_ref, *, dlat, rope):
    u = u_ref[...]
    lat = _rms_rows(u[:, :dlat], g_ref[...])
    r = _rope_pair(u[:, dlat:dlat + LANES], cos_ref[...], sin_ref[...])
    row_ref[:, 0:dlat] = lat
    row_ref[:, dlat:dlat + rope] = r[:, :rope]
    k_ref[:, 0:dlat] = lat.astype(k_ref.dtype)
    k_ref[:, dlat:dlat + LANES] = r.astype(k_ref.dtype)


def _kvprep(u, g_kv, cos, sin, rope):
    m = u.shape[0]
    dlat = g_kv.shape[1]
    tm = _pick(m, 512, 16)
    row = lambda w: pl.BlockSpec((tm, w), lambda i: (i, 0))
    return pl.pallas_call(
        functools.partial(_kvprep_kernel, dlat=dlat, rope=rope),
        grid=(m // tm,),
        in_specs=[row(u.shape[1]), pl.BlockSpec((1, dlat), lambda i: (0, 0)), row(LANES), row(LANES)],
        out_specs=[row(dlat + rope), row(dlat + LANES)],
        out_shape=[jax.ShapeDtypeStruct((m, dlat + rope), F32), jax.ShapeDtypeStruct((m, dlat + LANES), BF16)],
        compiler_params=_params("parallel"),
        name="mla_kvprep",
    )(u, g_kv, cos, sin)


def _online_softmax_step(m_ref, l_ref, acc_ref, s, v):
    m_prev = m_ref[...]
    m_new = jnp.maximum(m_prev, jnp.max(s, axis=1, keepdims=True))
    alpha = jnp.exp(m_prev - m_new)
    p = jnp.exp(s - m_new)
    l_ref[...] = alpha * l_ref[...] + jnp.sum(p, axis=1, keepdims=True)
    acc_ref[...] = alpha * acc_ref[...] + _dot(p.astype(BF16), v)
    m_ref[...] = m_new


def _causal_blocks(q_start, tq, tk, step):
    n_full = q_start // tk
    n_all = (q_start + tq + tk - 1) // tk

    def run(masked):
        def body(kb, carry):
            step(kb, masked)
            return carry
        return body

    lax.fori_loop(0, n_full, run(False), 0)
    lax.fori_loop(n_full, n_all, run(True), 0)


def _flash_mla_kernel(q_ref, k_ref, wv_ref, o_ref, m_ref, l_ref, acc_ref, *, tq, tk, nheads, dlat, dv):
    rows = nheads * tq
    q = q_ref[...].reshape(rows, q_ref.shape[-1])
    q_start = pl.program_id(1) * tq
    row_tok = q_start + lax.rem(lax.broadcasted_iota(jnp.int32, (rows, 1), 0), tq)
    m_ref[...] = jnp.full_like(m_ref, NEG)
    l_ref[...] = jnp.zeros_like(l_ref)
    acc_ref[...] = jnp.zeros_like(acc_ref)

    def step(kb, masked):
        off = pl.multiple_of(kb * tk, tk)
        k = k_ref[pl.ds(off, tk), :]
        s = _dot_nt(q, k)
        if masked:
            col = off + lax.broadcasted_iota(jnp.int32, (1, tk), 1)
            s = jnp.where(col <= row_tok, s, NEG)
        _online_softmax_step(m_ref, l_ref, acc_ref, s, k[:, :dlat])

    _causal_blocks(q_start, tq, tk, step)
    o = acc_ref[...] / l_ref[...]
    for h in range(nheads):
        oh = o[h * tq:(h + 1) * tq].astype(BF16)
        o_ref[:, h * dv:(h + 1) * dv] = _dot(oh, wv_ref[h]).astype(o_ref.dtype)


def _flash_mla(q, k, wv, batch, seq):
    nheads, _, dq = q.shape
    _, dlat, dv = wv.shape
    tq = _pick(seq, 128, 16)
    tk = _pick(seq, 256, 16)
    nq = seq // tq
    rows = nheads * tq
    return pl.pallas_call(
        functools.partial(_flash_mla_kernel, tq=tq, tk=tk, nheads=nheads, dlat=dlat, dv=dv),
        grid=(batch, nq),
        in_specs=[pl.BlockSpec((nheads, tq, dq), lambda b, i: (0, b * nq + i, 0)),
                  pl.BlockSpec((seq, dq), lambda b, i: (b, 0)),
                  pl.BlockSpec(wv.shape, lambda b, i: (0, 0, 0))],
        out_specs=pl.BlockSpec((tq, nheads * dv), lambda b, i: (b * nq + i, 0)),
        out_shape=jax.ShapeDtypeStruct((batch * seq, nheads * dv), BF16),
        scratch_shapes=[pltpu.VMEM((rows, 1), F32), pltpu.VMEM((rows, 1), F32), pltpu.VMEM((rows, dlat), F32)],
        compiler_params=_params("parallel", "parallel"),
        name="flash_mla",
    )(q, k, wv)


def _diff_lambda(lam_ref, lam_init):
    lp = lam_ref[...]
    s01 = jnp.sum(lp[0:1] * lp[1:2], axis=1, keepdims=True)
    s23 = jnp.sum(lp[2:3] * lp[3:4], axis=1, keepdims=True)
    return jnp.exp(s01) - jnp.exp(s23) + lam_init


def _diff_combine(o1, o2, lam, g, lam_init):
    return _rms_rows(o1 - lam * o2, g) * (1.0 - lam_init)


def _flash_diff_kernel(q_ref, k_ref, v_ref, lam_ref, g_ref, o_ref, m_ref, l_ref, acc_ref, *, tq, tk, nh, dd, lam_init):
    rows = nh * tq
    q = q_ref[...]
    qs = [jnp.concatenate([q[:, (2 * h + j) * dd:(2 * h + j + 1) * dd] for h in range(nh)], axis=0) for j in range(2)]
    q_start = pl.program_id(1) * tq
    row_tok = q_start + lax.rem(lax.broadcasted_iota(jnp.int32, (rows, 1), 0), tq)
    m_ref[...] = jnp.full_like(m_ref, NEG)
    l_ref[...] = jnp.zeros_like(l_ref)
    acc_ref[...] = jnp.zeros_like(acc_ref)

    def step(kb, masked):
        off = pl.multiple_of(kb * tk, tk)
        k = k_ref[pl.ds(off, tk), :].astype(BF16)
        v = v_ref[pl.ds(off, tk), :].astype(BF16)
        for j in range(2):
            s = _dot_nt(qs[j], k[:, j * dd:(j + 1) * dd])
            if masked:
                col = off + lax.broadcasted_iota(jnp.int32, (1, tk), 1)
                s = jnp.where(col <= row_tok, s, NEG)
            _online_softmax_step(m_ref.at[j], l_ref.at[j], acc_ref.at[j], s, v)

    _causal_blocks(q_start, tq, tk, step)
    lam = _diff_lambda(lam_ref, lam_init)
    r = _diff_combine(acc_ref[0] / l_ref[0], acc_ref[1] / l_ref[1], lam, g_ref[...], lam_init)
    for h in range(nh):
        o_ref[:, h * 2 * dd:(h + 1) * 2 * dd] = r[h * tq:(h + 1) * tq].astype(o_ref.dtype)


def _flash_diff(q, k, v, lam_p, g_sub, batch, seq, lam_init):
    dd = k.shape[1] // 2
    nh = q.shape[1] // (2 * dd)
    tq = _pick(seq, 128, 16)
    tk = _pick(seq, 256, 16)
    nq = seq // tq
    rows = nh * tq
    return pl.pallas_call(
        functools.partial(_flash_diff_kernel, tq=tq, tk=tk, nh=nh, dd=dd, lam_init=lam_init),
        grid=(batch, nq),
        in_specs=[pl.BlockSpec((tq, q.shape[1]), lambda b, i: (b * nq + i, 0)),
                  pl.BlockSpec((seq, 2 * dd), lambda b, i: (b, 0)),
                  pl.BlockSpec((seq, 2 * dd), lambda b, i: (b, 0)),
                  pl.BlockSpec(lam_p.shape, lambda b, i: (0, 0)),
                  pl.BlockSpec(g_sub.shape, lambda b, i: (0, 0))],
        out_specs=pl.BlockSpec((tq, nh * 2 * dd), lambda b, i: (b * nq + i, 0)),
        out_shape=jax.ShapeDtypeStruct((batch * seq, nh * 2 * dd), BF16),
        scratch_shapes=[pltpu.VMEM((2, rows, 1), F32), pltpu.VMEM((2, rows, 1), F32), pltpu.VMEM((2, rows, 2 * dd), F32)],
        compiler_params=_params("parallel", "parallel"),
        name="flash_diff",
    )(q, k, v, lam_p, g_sub)


def _single_key_update(m_ref, l_ref, acc_ref, s_new, v_new):
    m_prev = m_ref[...]
    m_new = jnp.maximum(m_prev, s_new)
    alpha = jnp.exp(m_prev - m_new)
    p = jnp.exp(s_new - m_new)
    l = alpha * l_ref[...] + p
    acc = alpha * acc_ref[...] + p.astype(BF16).astype(F32) * v_new
    return acc / l


def _decode_attn_kernel(pt_ref, q_ref, dq_ref, kn_ref, dkn_ref, dvn_ref, lam_ref, g_ref, *rest,
                        pps, page, nh_d, dd, dlat, row, lam_init):
    del pt_ref
    mla_pages, dk_pages, dv_pages = rest[:pps], rest[pps:2 * pps], rest[2 * pps:3 * pps]
    olat_ref, od_ref = rest[3 * pps:3 * pps + 2]
    kbuf, m1, l1, a1, m2, l2, a2 = rest[3 * pps + 2:]
    c = pl.program_id(1)

    @pl.when(c == 0)
    def _():
        kbuf[...] = jnp.zeros_like(kbuf)
        for m_ref, l_ref, a_ref in ((m1, l1, a1), (m2, l2, a2)):
            m_ref[...] = jnp.full_like(m_ref, NEG)
            l_ref[...] = jnp.zeros_like(l_ref)
            a_ref[...] = jnp.zeros_like(a_ref)

    for r in range(pps):
        kbuf[r * page:(r + 1) * page, 0:row] = mla_pages[r][...]
    k = kbuf[...].astype(BF16)
    q = q_ref[...]
    _online_softmax_step(m1, l1, a1, _dot_nt(q, k), k[:, :dlat])

    dk = jnp.concatenate([p[...] for p in dk_pages], axis=0).astype(BF16)
    dv = jnp.concatenate([p[...] for p in dv_pages], axis=0).astype(BF16)
    dq = dq_ref[...]
    first = lax.broadcasted_iota(jnp.int32, (2 * nh_d, 1), 0) < nh_d
    s2 = jnp.where(first, _dot_nt(dq, dk[:, :dd]), _dot_nt(dq, dk[:, dd:]))
    _online_softmax_step(m2, l2, a2, s2, dv)

    @pl.when(c == pl.num_programs(1) - 1)
    def _():
        kn = kn_ref[...].astype(F32)
        sn = jnp.sum(q.astype(F32) * kn, axis=1, keepdims=True)
        olat_ref[...] = _single_key_update(m1, l1, a1, sn, kn[:, :dlat])
        dkn = dkn_ref[...].astype(BF16).astype(F32)
        dvn = dvn_ref[...].astype(BF16).astype(F32)
        dqf = dq.astype(F32)
        sn2 = jnp.where(first, jnp.sum(dqf * dkn[:, :dd], axis=1, keepdims=True),
                        jnp.sum(dqf * dkn[:, dd:], axis=1, keepdims=True))
        on = _single_key_update(m2, l2, a2, sn2, dvn)
        lam = _diff_lambda(lam_ref, lam_init)
        od_ref[...] = _diff_combine(on[:nh_d], on[nh_d:], lam, g_ref[...], lam_init).astype(od_ref.dtype)


def _decode_attn(page_table, q, dq, k_new, dk_new, dv_new, lam_p, g_sub, cache_mla, cache_dk, cache_dv, lam_init):
    nb, n_pages = page_table.shape
    n_pool, page, row = cache_mla.shape
    nheads, dq_w = q.shape[1:]
    dd = cache_dv.shape[-1] // 2
    nh_d = dq.shape[1] // 2
    dlat = dq_w - LANES
    pps = _pick(n_pages, 16, 1)
    cache_dk = cache_dk.reshape(n_pool, page, 2 * dd)

    def page_spec(width, r):
        return pl.BlockSpec((None, page, width), lambda b, c, pt: (pt[b * n_pages + c * pps + r], 0, 0))

    per_seq = lambda a: pl.BlockSpec((None,) + a.shape[1:], lambda b, c, pt: (b, 0, 0))
    full = lambda a: pl.BlockSpec(a.shape, lambda b, c, pt: (0, 0))
    grid_spec = pltpu.PrefetchScalarGridSpec(
        num_scalar_prefetch=1,
        grid=(nb, n_pages // pps),
        in_specs=([per_seq(q), per_seq(dq), per_seq(k_new), per_seq(dk_new), per_seq(dv_new), full(lam_p), full(g_sub)]
                  + [page_spec(row, r) for r in range(pps)]
                  + [page_spec(2 * dd, r) for r in range(pps)]
                  + [page_spec(2 * dd, r) for r in range(pps)]),
        out_specs=[pl.BlockSpec((None, nheads, dlat), lambda b, c, pt: (b, 0, 0)),
                   pl.BlockSpec((None, nh_d, 2 * dd), lambda b, c, pt: (b, 0, 0))],
        scratch_shapes=[pltpu.VMEM((pps * page, dq_w), F32),
                        pltpu.VMEM((nheads, 1), F32), pltpu.VMEM((nheads, 1), F32), pltpu.VMEM((nheads, dlat), F32),
                        pltpu.VMEM((2 * nh_d, 1), F32), pltpu.VMEM((2 * nh_d, 1), F32), pltpu.VMEM((2 * nh_d, 2 * dd), F32)],
    )
    return pl.pallas_call(
        functools.partial(_decode_attn_kernel, pps=pps, page=page, nh_d=nh_d, dd=dd, dlat=dlat, row=row, lam_init=lam_init),
        grid_spec=grid_spec,
        out_shape=[jax.ShapeDtypeStruct((nb, nheads, dlat), F32), jax.ShapeDtypeStruct((nb, nh_d, 2 * dd), BF16)],
        compiler_params=_params("parallel", "arbitrary"),
        name="decode_attn",
    )(page_table.reshape(-1), q, dq, k_new, dk_new, dv_new, lam_p, g_sub,
      *([cache_mla] * pps), *([cache_dk] * pps), *([cache_dv] * pps))


def _head_proj_kernel(x_ref, w_ref, o_ref):
    o_ref[...] = _dot(x_ref[...].astype(BF16), w_ref[...]).astype(o_ref.dtype)


def _head_proj(x, w):
    nheads, n, c = x.shape
    v = w.shape[2]
    return pl.pallas_call(
        _head_proj_kernel,
        grid=(nheads,),
        in_specs=[pl.BlockSpec((None, n, c), lambda h: (h, 0, 0)), pl.BlockSpec((None, c, v), lambda h: (h, 0, 0))],
        out_specs=pl.BlockSpec((n, v), lambda h: (0, h)),
        out_shape=jax.ShapeDtypeStruct((n, nheads * v), BF16),
        compiler_params=_params("parallel"),
        name="head_proj",
    )(x, w)


def _softplus(x):
    return jnp.maximum(x, 0.0) + jnp.log1p(jnp.exp(-jnp.abs(x)))


def _dtprep_kernel(raw_ref, bias_ref, alog_ref, dt_ref, da_ref, cum_ref):
    dt = _softplus(raw_ref[...] + bias_ref[...])
    da = dt * -jnp.exp(alog_ref[...])
    dt_ref[...] = dt
    da_ref[...] = da
    rows = da.shape[0]
    row = lax.broadcasted_iota(jnp.int32, da.shape, 0)
    cum = da
    shift = 1
    while shift < rows:
        cum = cum + jnp.where(row >= shift, pltpu.roll(cum, shift, 0), 0.0)
        shift *= 2
    cum_ref[...] = cum


def _dtprep(raw, bias, a_log, chunk):
    m, nh = raw.shape
    spec = pl.BlockSpec((chunk, nh), lambda i: (i, 0))
    vec = pl.BlockSpec((1, nh), lambda i: (0, 0))
    shape = jax.ShapeDtypeStruct((m, nh), F32)
    return pl.pallas_call(
        _dtprep_kernel,
        grid=(m // chunk,),
        in_specs=[spec, vec, vec],
        out_specs=[spec, spec, spec],
        out_shape=[shape, shape, shape],
        compiler_params=_params("parallel"),
        name="ssd_dtprep",
    )(raw, bias.reshape(1, nh), a_log.reshape(1, nh))


def _conv_kernel(x_ref, halo_ref, w_ref, b_ref, o_ref, buf, *, tt, width, halo):
    j = pl.program_id(1)
    buf[0:halo, :] = jnp.where(j == 0, 0.0, halo_ref[...])
    buf[halo:halo + tt, :] = x_ref[...]
    acc = b_ref[...] + buf[pl.ds(halo - (width - 1), tt), :] * w_ref[0:1, :]
    for k in range(1, width):
        acc = acc + buf[pl.ds(halo - (width - 1) + k, tt), :] * w_ref[k:k + 1, :]
    o_ref[...] = _silu(acc)


def _conv_prompt(x, w, b, batch, seq):
    cdim = x.shape[1]
    width = w.shape[0]
    halo = 8
    tt = _pick(seq, 256, 8)
    tc = _pick(cdim, 2048, LANES)
    nt = seq // tt
    return pl.pallas_call(
        functools.partial(_conv_kernel, tt=tt, width=width, halo=halo),
        grid=(batch, nt, cdim // tc),
        in_specs=[pl.BlockSpec((tt, tc), lambda b_, j, c: (b_ * nt + j, c)),
                  pl.BlockSpec((halo, tc), lambda b_, j, c: (jnp.maximum((b_ * nt + j) * (tt // halo) - 1, 0), c)),
                  pl.BlockSpec((width, tc), lambda b_, j, c: (0, c)),
                  pl.BlockSpec((1, tc), lambda b_, j, c: (0, c))],
        out_specs=pl.BlockSpec((tt, tc), lambda b_, j, c: (b_ * nt + j, c)),
        out_shape=jax.ShapeDtypeStruct((batch * seq, cdim), F32),
        scratch_shapes=[pltpu.VMEM((halo + tt, tc), F32)],
        compiler_params=_params("parallel", "parallel", "parallel"),
        name="ssd_conv",
    )(x, x, w, b.reshape(1, cdim))


def _conv_decode_kernel(s_ref, x_ref, w_ref, b_ref, o_ref, *, width):
    acc = b_ref[...] + s_ref[0] * w_ref[0:1, :]
    for k in range(1, width - 1):
        acc = acc + s_ref[k] * w_ref[k:k + 1, :]
    o_ref[...] = _silu(acc + x_ref[...] * w_ref[width - 1:width, :])


def _conv_decode(state_t, x, w, b):
    n, cdim = x.shape
    width = w.shape[0]
    tc = _pick(cdim, 2048, LANES)
    return pl.pallas_call(
        functools.partial(_conv_decode_kernel, width=width),
        grid=(cdim // tc,),
        in_specs=[pl.BlockSpec((width - 1, n, tc), lambda c: (0, 0, c)),
                  pl.BlockSpec((n, tc), lambda c: (0, c)),
                  pl.BlockSpec((width, tc), lambda c: (0, c)),
                  pl.BlockSpec((1, tc), lambda c: (0, c))],
        out_specs=pl.BlockSpec((n, tc), lambda c: (0, c)),
        out_shape=jax.ShapeDtypeStruct((n, cdim), F32),
        compiler_params=_params("parallel"),
        name="ssd_conv_decode",
    )(state_t, x, w, b.reshape(1, cdim))


def _gate_norm(y, z, g):
    return _rms_rows(y * _silu(z), g)


def _ssd_chunk_kernel(x_ref, b_ref, c_ref, z_ref, cum_ref, cumt_ref, dt_ref, dtt_ref, dskip_ref, g_ref,
                      y_ref, hout_ref, h_ref, y_scr, *, chunk, hpg, hd):
    ci = pl.program_id(2)

    @pl.when(ci == 0)
    def _():
        h_ref[...] = jnp.zeros_like(h_ref)

    x = x_ref[...]
    bmat = b_ref[...]
    cmat = c_ref[...].astype(BF16)
    cb = _dot_nt(cmat, bmat.astype(BF16))
    h_all = h_ref[...]
    y_inter = _dot_nt(cmat, h_all.astype(BF16))
    cum = cum_ref[...]
    dt = dt_ref[...]
    causal = (lax.broadcasted_iota(jnp.int32, (chunk, chunk), 0) >= lax.broadcasted_iota(jnp.int32, (chunk, chunk), 1))
    for r in range(hpg):
        sl = slice(r * hd, (r + 1) * hd)
        xr = x[:, sl].astype(BF16)
        ac = cum[:, r:r + 1]
        seg = ac - cumt_ref[r:r + 1, :]
        wgt = cb * jnp.exp(jnp.where(causal, seg, -jnp.inf)) * dtt_ref[r:r + 1, :]
        y_scr[:, sl] = _dot(wgt.astype(BF16), xr) + y_inter[:, sl] * jnp.exp(ac)
        last = ac[chunk - 1:chunk, :]
        to_end = jnp.exp(last - ac) * dt[:, r:r + 1]
        h_ref[sl, :] = h_all[sl, :] * jnp.exp(last) + _dot_tn(xr, (bmat * to_end).astype(BF16))
    y = y_scr[...] + dskip_ref[...] * x
    y_ref[...] = _gate_norm(y, z_ref[...], g_ref[...]).astype(y_ref.dtype)

    @pl.when(ci == pl.num_programs(2) - 1)
    def _():
        hout_ref[...] = h_ref[...]


def _ssd_prompt(xbc, z, cum_g, cum_t, dt_g, dt_t, dskip, g_norm, batch, seq, chunk, ngroups, nstate):
    d_inner = z.shape[1]
    gw = d_inner // ngroups
    hpg = cum_g.shape[2]
    hd = gw // hpg
    nc = seq // chunk
    xoff = d_inner // nstate
    rows = lambda b, g, c: b * nc + c
    return pl.pallas_call(
        functools.partial(_ssd_chunk_kernel, chunk=chunk, hpg=hpg, hd=hd),
        grid=(batch, ngroups, nc),
        in_specs=[pl.BlockSpec((chunk, gw), lambda b, g, c: (rows(b, g, c), g)),
                  pl.BlockSpec((chunk, nstate), lambda b, g, c: (rows(b, g, c), xoff + g)),
                  pl.BlockSpec((chunk, nstate), lambda b, g, c: (rows(b, g, c), xoff + ngroups + g)),
                  pl.BlockSpec((chunk, gw), lambda b, g, c: (rows(b, g, c), g)),
                  pl.BlockSpec((None, chunk, hpg), lambda b, g, c: (g, rows(b, g, c), 0)),
                  pl.BlockSpec((hpg, chunk), lambda b, g, c: (g, rows(b, g, c))),
                  pl.BlockSpec((None, chunk, hpg), lambda b, g, c: (g, rows(b, g, c), 0)),
                  pl.BlockSpec((hpg, chunk), lambda b, g, c: (g, rows(b, g, c))),
                  pl.BlockSpec((1, gw), lambda b, g, c: (0, g)),
                  pl.BlockSpec((1, gw), lambda b, g, c: (0, g))],
        out_specs=[pl.BlockSpec((chunk, gw), lambda b, g, c: (rows(b, g, c), g)),
                   pl.BlockSpec((None, gw, nstate), lambda b, g, c: (b * ngroups + g, 0, 0))],
        out_shape=[jax.ShapeDtypeStruct((batch * seq, d_inner), BF16),
                   jax.ShapeDtypeStruct((batch * ngroups, gw, nstate), F32)],
        scratch_shapes=[pltpu.VMEM((gw, nstate), F32), pltpu.VMEM((chunk, gw), F32)],
        compiler_params=_params("parallel", "parallel", "arbitrary"),
        name="ssd_chunk_scan",
    )(xbc, xbc, xbc, z, cum_g, cum_t, dt_g, dt_t, dskip, g_norm)


def _ssd_step_kernel(h_ref, xt_ref, b_ref, c_ref, dt_ref, da_ref, hout_ref, yt_ref, *, sb, hpg):
    lane = lax.broadcasted_iota(jnp.int32, yt_ref.shape[1:], 1)
    for s in range(sb):
        dt_row = dt_ref[s:s + 1, :]
        xdt = xt_ref[s] * dt_row
        decay = jnp.broadcast_to(jnp.exp(da_ref[s:s + 1, :]), xdt.shape)
        brow = b_ref[s:s + 1, :]
        crow = c_ref[s:s + 1, :]
        yt = jnp.zeros(xdt.shape, F32)
        for r in range(hpg):
            hn = h_ref[s, r] * decay[:, r:r + 1] + xdt[:, r:r + 1] * brow
            hout_ref[s, r] = hn
            yt = jnp.where(lane == r, jnp.sum(hn * crow, axis=1, keepdims=True), yt)
        yt_ref[s] = yt


def _ssd_step(h, xt, b, c, dt, da):
    ngroups, n, hd, hpg = xt.shape
    nstate = b.shape[2]
    sb = _pick(n, 8, 8)
    h = h.reshape(n, ngroups * hpg, hd, nstate)
    hspec = pl.BlockSpec((sb, hpg, hd, nstate), lambda i, g: (i, g, 0, 0))
    vec = lambda w: pl.BlockSpec((None, sb, w), lambda i, g: (g, i, 0))
    xspec = pl.BlockSpec((None, sb, hd, hpg), lambda i, g: (g, i, 0, 0))
    return pl.pallas_call(
        functools.partial(_ssd_step_kernel, sb=sb, hpg=hpg),
        grid=(n // sb, ngroups),
        in_specs=[hspec, xspec, vec(nstate), vec(nstate), vec(hpg), vec(hpg)],
        out_specs=[hspec, xspec],
        out_shape=[jax.ShapeDtypeStruct(h.shape, F32), jax.ShapeDtypeStruct(xt.shape, F32)],
        compiler_params=_params("parallel", "parallel"),
        name="ssd_step",
    )(h, xt, b, c, dt, da)


def _ssd_gate_kernel(y_ref, x_ref, z_ref, dskip_ref, g_ref, o_ref):
    y = y_ref[...] + dskip_ref[...] * x_ref[...]
    o_ref[...] = _gate_norm(y, z_ref[...], g_ref[...]).astype(o_ref.dtype)


def _ssd_gate(y, xbc, z, dskip, g_norm, ngroups):
    n, d_inner = y.shape
    gw = d_inner // ngroups
    blk = pl.BlockSpec((n, gw), lambda g: (0, g))
    vec = pl.BlockSpec((1, gw), lambda g: (0, g))
    return pl.pallas_call(
        _ssd_gate_kernel,
        grid=(ngroups,),
        in_specs=[blk, blk, blk, vec, vec],
        out_specs=blk,
        out_shape=jax.ShapeDtypeStruct((n, d_inner), BF16),
        compiler_params=_params("parallel"),
        name="ssd_gate",
    )(y, xbc, z, dskip, g_norm)


def _rot_cols(w):
    half = w.shape[-1] // 2
    return jnp.concatenate([-w[..., half:], w[..., :half]], axis=-1)


def _attn_layer(h, layer, mp, seq, batch, page_table, caches, cos, sin, w_in, g_q, g_kv, w_qb, w_kvb, lam_p, g_sub, w_out):
    cache_mla, cache_dk, cache_dv = caches
    q_lora = g_q.shape[0]
    dlat = g_kv.shape[0]
    rope = cache_mla.shape[-1] - dlat
    dd = cache_dv.shape[-1] // 2
    nheads = w_qb.shape[1]
    nope = w_qb.shape[2] - rope
    dv = w_kvb.shape[2] - nope
    assert 2 * rope == LANES, "the rope rotation is done inside one 128-lane chunk"
    lam_init = 0.8 - 0.6 * math.exp(-0.3 * layer)
    nd = h.shape[0] - mp

    o_ckv, o_kpe, o_dq = q_lora, q_lora + dlat, q_lora + dlat + rope
    o_dk = w_in.shape[1] - 4 * dd
    w_cq = w_in[:, :q_lora].astype(BF16)
    w_kpe = w_in[:, o_kpe:o_dq]
    w_ckr = jnp.concatenate([w_in[:, o_ckv:o_kpe], w_kpe, _rot_cols(w_kpe)], axis=1).astype(BF16)
    w_dq = w_in[:, o_dq:o_dk].astype(BF16)
    w_dk = w_in[:, o_dk:o_dk + 2 * dd].astype(BF16)
    w_dv = w_in[:, o_dk + 2 * dd:].astype(BF16)
    w_pe = w_qb[..., nope:]
    wq = jnp.concatenate([w_qb[..., :nope].reshape(q_lora, nheads * nope),
                          jnp.concatenate([w_pe, _rot_cols(w_pe)], axis=-1).reshape(q_lora, nheads * LANES)],
                         axis=1).astype(BF16)
    wk = jnp.transpose(w_kvb[..., :nope], (1, 2, 0)).astype(BF16)
    wv = jnp.transpose(w_kvb[..., nope:], (1, 0, 2)).astype(BF16)

    cq = _matmul(h, w_cq)
    ckr = _matmul(h, w_ckr)
    dq = _matmul(h, w_dq, out_dtype=BF16, scale=dd ** -0.5)
    dk = _matmul(h, w_dk)
    dv_rows = _matmul(h, w_dv)

    q = _qprep(cq, g_q.reshape(1, -1), wq, wk, cos, sin, (nope + rope) ** -0.5)
    mla_row, k_pad = _kvprep(ckr, g_kv.reshape(1, -1), cos, sin, rope)

    g_sub2 = g_sub.reshape(1, -1)
    o_mla_p = _flash_mla(q, k_pad, wv, batch, seq)
    o_d_p = _flash_diff(dq, dk, dv_rows, lam_p, g_sub2, batch, seq, lam_init)

    q_s = jnp.transpose(q[:, mp:], (1, 0, 2))
    nh_d = dq.shape[1] // (2 * dd)
    dq_s = jnp.transpose(dq[mp:].reshape(nd, nh_d, 2, dd), (0, 2, 1, 3)).reshape(nd, 2 * nh_d, dd)
    o_lat_s, o_d_s = _decode_attn(page_table, q_s, dq_s, k_pad[mp:, None], dk[mp:, None], dv_rows[mp:, None],
                                  lam_p, g_sub2, cache_mla, cache_dk, cache_dv, lam_init)
    o_mla_s = _head_proj(jnp.transpose(o_lat_s, (1, 0, 2)), wv)

    o = jnp.concatenate([jnp.concatenate([o_mla_p, o_d_p], axis=1),
                         jnp.concatenate([o_mla_s, o_d_s.reshape(nd, -1)], axis=1)], axis=0)
    out = _matmul(o, w_out.astype(BF16))
    new = (mla_row[:mp].reshape(batch, seq, -1), dk[:mp].reshape(batch, seq, 2, dd), dv_rows[:mp].reshape(batch, seq, -1),
           mla_row[mp:].reshape(nd, 1, -1), dk[mp:].reshape(nd, 1, 2, dd), dv_rows[mp:].reshape(nd, 1, -1))
    return out, new


def _ssd_layer(h, mp, seq, batch, conv_state, ssm_state, w_in, conv_w, conv_b, dt_bias, a_log, d_skip, g_norm, w_out):
    nd, nheads, hd, nstate = ssm_state.shape
    d_inner = nheads * hd
    cdim = conv_w.shape[1]
    ngroups = (cdim - d_inner) // (2 * nstate)
    hpg = nheads // ngroups
    chunk = _pick(seq, 256, 8)

    z = _matmul(h, w_in[:, :d_inner].astype(BF16))
    xbc = _matmul(h, w_in[:, d_inner:d_inner + cdim].astype(BF16))
    dt_raw = _matmul(h, w_in[:, d_inner + cdim:].astype(BF16))

    dskip = jnp.repeat(d_skip, hd).reshape(1, d_inner)
    g2 = g_norm.reshape(1, d_inner)

    dt_p, _, cum_p = _dtprep(dt_raw[:mp], dt_bias, a_log, chunk)
    by_group = lambda a: jnp.transpose(a.reshape(-1, ngroups, hpg), (1, 0, 2))
    xbc_act = _conv_prompt(xbc[:mp], conv_w, conv_b, batch, seq)
    y_p, h_p = _ssd_prompt(xbc_act, z, by_group(cum_p), cum_p.T, by_group(dt_p), dt_p.T, dskip, g2,
                           batch, seq, chunk, ngroups, nstate)
    conv_p = xbc[:mp].reshape(batch, seq, cdim)[:, seq - (conv_w.shape[0] - 1):]
    ssm_p = h_p.reshape(batch, nheads, hd, nstate)

    xbc_s = xbc[mp:]
    dt_s, da_s, _ = _dtprep(dt_raw[mp:], dt_bias, a_log, nd)
    act_s = _conv_decode(jnp.transpose(conv_state, (1, 0, 2)), xbc_s, conv_w, conv_b)
    xt = jnp.transpose(act_s[:, :d_inner].reshape(nd, ngroups, hpg, hd), (1, 0, 3, 2))
    b_s = jnp.transpose(act_s[:, d_inner:d_inner + ngroups * nstate].reshape(nd, ngroups, nstate), (1, 0, 2))
    c_s = jnp.transpose(act_s[:, d_inner + ngroups * nstate:].reshape(nd, ngroups, nstate), (1, 0, 2))
    ssm_s, yt = _ssd_step(ssm_state, xt, b_s, c_s, by_group(dt_s), by_group(da_s))
    y_s = jnp.transpose(yt, (1, 0, 3, 2)).reshape(nd, d_inner)
    y_s = _ssd_gate(y_s, act_s, z[mp:], dskip, g2, ngroups)
    conv_s = jnp.concatenate([conv_state[:, 1:], xbc_s[:, None]], axis=1)

    out = _matmul(jnp.concatenate([y_p, y_s], axis=0), w_out.astype(BF16))
    return out, (conv_p, ssm_p, conv_s, ssm_s)


def kernel(x_prompt, x_sample, cache_mla_l0, cache_dk_l0, cache_dv_l0, state_conv_l1, state_ssm_l1, cache_mla_l2, cache_dk_l2, cache_dv_l2, state_conv_l3, state_ssm_l3, page_table, norm_mix_pre, norm_mix_post, norm_ffn_pre, norm_ffn_post, ffn_w_gate, ffn_w_up, ffn_w_down, attn_w_in, mla_g_q, mla_g_kv, mla_w_qb, mla_w_kvb, diff_lambda, diff_g_sub, attn_w_out, ssm_w_in, ssm_conv_w, ssm_conv_b, ssm_dt_bias, ssm_a_log, ssm_d, ssm_g_norm, ssm_w_out):
    batch, seq, d_model = x_prompt.shape
    nd, dec_seq, _ = x_sample.shape
    assert dec_seq == 1, "decode rows are one new token per sequence"
    mp = batch * seq
    depth = norm_mix_pre.shape[0]
    attn_caches = {0: (cache_mla_l0, cache_dk_l0, cache_dv_l0), 2: (cache_mla_l2, cache_dk_l2, cache_dv_l2)}
    ssm_states = {1: (state_conv_l1, state_ssm_l1), 3: (state_conv_l3, state_ssm_l3)}

    rope = cache_mla_l0.shape[-1] - mla_g_kv.shape[1]
    half = rope // 2
    past_len = page_table.shape[1] * cache_mla_l0.shape[1]
    pos = jnp.concatenate([jnp.tile(jnp.arange(seq), batch), jnp.full((nd,), past_len)])
    inv = jnp.exp(-math.log(ROPE_BASE) * jnp.arange(half, dtype=F32) / half)
    ang = pos.astype(F32)[:, None] * inv[None, :]
    pad = jnp.zeros((mp + nd, LANES - rope), F32)
    cos = jnp.concatenate([jnp.cos(ang), jnp.cos(ang), pad], axis=1)
    sin = jnp.concatenate([jnp.sin(ang), jnp.sin(ang), pad], axis=1)

    d_ff = ffn_w_gate.shape[2]
    ff_pad = -d_ff % FFN_ALIGN

    x = jnp.concatenate([x_prompt.reshape(mp, d_model), x_sample.reshape(nd, d_model)], axis=0)
    h = _rms(x, norm_mix_pre[0])
    new = {}
    for layer in range(depth):
        j = layer // 2
        if layer % 2 == 0:
            o, new[layer] = _attn_layer(h, layer, mp, seq, batch, page_table, attn_caches[layer], cos, sin,
                                        attn_w_in[j], mla_g_q[j], mla_g_kv[j], mla_w_qb[j], mla_w_kvb[j],
                                        diff_lambda[j], diff_g_sub[j], attn_w_out[j])
        else:
            o, new[layer] = _ssd_layer(h, mp, seq, batch, *ssm_states[layer], ssm_w_in[j], ssm_conv_w[j], ssm_conv_b[j],
                                       ssm_dt_bias[j], ssm_a_log[j], ssm_d[j], ssm_g_norm[j], ssm_w_out[j])
        x, h = _add_rms(x, o, norm_mix_post[layer], norm_ffn_pre[layer])
        wg = jnp.pad(ffn_w_gate[layer], ((0, 0), (0, ff_pad))).astype(BF16)
        wu = jnp.pad(ffn_w_up[layer], ((0, 0), (0, ff_pad))).astype(BF16)
        wd = jnp.pad(ffn_w_down[layer], ((0, ff_pad), (0, 0))).astype(BF16)
        o = _matmul(_swiglu_up(h, wg, wu), wd, bk=2816)
        x, h = _add_rms(x, o, norm_ffn_post[layer], norm_mix_pre[layer + 1] if layer + 1 < depth else None)
    outs = [x[:mp].reshape(batch, seq, d_model), x[mp:].reshape(nd, 1, d_model)]
    for layer in range(depth):
        outs.extend(new[layer])
    return tuple(outs)
```

```python
import functools
import math

import jax
import jax.numpy as jnp
from jax import lax
from jax.experimental import pallas as pl
from jax.experimental.pallas import tpu as pltpu

F32 = jnp.float32
BF16 = jnp.bfloat16
NORM_EPS = 1e-6
ROPE_BASE = 10000.0
NEG = -1e30
LANES = 128
VMEM_LIMIT = 56 * 1024 * 1024
FFN_ALIGN = 1024
FLASH_TQ = 128
FLASH_TK = 512
DECODE_PAGES = 32


def _pick(n, target, align):
    if n <= target:
        return n
    d = target - target % align
    while d >= align:
        if n % d == 0:
            return d
        d -= align
    return n


def _params(*sem):
    return pltpu.CompilerParams(dimension_semantics=sem, vmem_limit_bytes=VMEM_LIMIT)


def _rms_rows(x, g):
    return x * lax.rsqrt(jnp.mean(x * x, axis=-1, keepdims=True) + NORM_EPS) * g


def _silu(x):
    return x * jax.nn.sigmoid(x)


def _dot(a, b):
    return jnp.dot(a, b, preferred_element_type=F32)


def _dot_nt(a, b):
    return lax.dot_general(a, b, (((1,), (1,)), ((), ())), preferred_element_type=F32)


def _dot_tn(a, b):
    return lax.dot_general(a, b, (((0,), (0,)), ((), ())), preferred_element_type=F32)


def _rms_kernel(x_ref, g_ref, o_ref):
    o_ref[...] = _rms_rows(x_ref[...], g_ref[...]).astype(o_ref.dtype)


def _rms(x, g, out_dtype=BF16):
    m, d = x.shape
    tm = _pick(m, 512, 16)
    return pl.pallas_call(
        _rms_kernel,
        grid=(m // tm,),
        in_specs=[pl.BlockSpec((tm, d), lambda i: (i, 0)), pl.BlockSpec((1, d), lambda i: (0, 0))],
        out_specs=pl.BlockSpec((tm, d), lambda i: (i, 0)),
        out_shape=jax.ShapeDtypeStruct((m, d), out_dtype),
        compiler_params=_params("parallel"),
        name="rms",
    )(x, g.reshape(1, d))


def _add_rms_kernel(x_ref, o_ref, gpost_ref, *rest, with_next):
    xn = x_ref[...] + _rms_rows(o_ref[...], gpost_ref[...])
    if with_next:
        gnext_ref, xo_ref, h_ref = rest
        h_ref[...] = _rms_rows(xn, gnext_ref[...]).astype(h_ref.dtype)
    else:
        (xo_ref,) = rest
    xo_ref[...] = xn


def _add_rms(x, o, g_post, g_next=None):
    m, d = x.shape
    tm = _pick(m, 256, 16)
    row = pl.BlockSpec((tm, d), lambda i: (i, 0))
    vec = pl.BlockSpec((1, d), lambda i: (0, 0))
    with_next = g_next is not None
    in_specs = [row, row, vec] + ([vec] if with_next else [])
    args = [x, o, g_post.reshape(1, d)] + ([g_next.reshape(1, d)] if with_next else [])
    out_shape = [jax.ShapeDtypeStruct((m, d), F32)] + ([jax.ShapeDtypeStruct((m, d), BF16)] if with_next else [])
    out = pl.pallas_call(
        functools.partial(_add_rms_kernel, with_next=with_next),
        grid=(m // tm,),
        in_specs=in_specs,
        out_specs=[row] * len(out_shape),
        out_shape=out_shape,
        compiler_params=_params("parallel"),
        name="add_rms",
    )(*args)
    return (out[0], out[1]) if with_next else (out[0], None)


def _mm_kernel(x_ref, w_ref, o_ref, *scratch, nk, scale):
    def finish(acc):
        return (acc * scale if scale != 1.0 else acc).astype(o_ref.dtype)

    if nk == 1:
        o_ref[...] = finish(_dot(x_ref[...], w_ref[...]))
        return
    acc_ref = scratch[0] if scratch else o_ref
    k = pl.program_id(2)

    @pl.when(k == 0)
    def _():
        acc_ref[...] = jnp.zeros_like(acc_ref)

    acc_ref[...] += _dot(x_ref[...], w_ref[...])
    if scratch or scale != 1.0:
        @pl.when(k == nk - 1)
        def _():
            o_ref[...] = finish(acc_ref[...])


def _matmul(x, w, out_dtype=F32, scale=1.0, layer=None, cols=None, bm=1040, bn=1024, bk=4096):
    m, kdim = x.shape
    n0, n = cols if cols is not None else (0, w.shape[-1])
    bm = _pick(m, bm, 16)
    bn = _pick(math.gcd(n, n0) if n0 else n, bn, LANES)
    bk = _pick(kdim, bk, LANES)
    nk = kdim // bk
    j0 = n0 // bn
    use_scratch = nk > 1 and out_dtype != F32
    if w.ndim == 3:
        wspec = pl.BlockSpec((None, bk, bn), lambda i, j, k: (layer, k, j0 + j))
    else:
        wspec = pl.BlockSpec((bk, bn), lambda i, j, k: (k, j0 + j))
    return pl.pallas_call(
        functools.partial(_mm_kernel, nk=nk, scale=scale),
        grid=(m // bm, n // bn, nk),
        in_specs=[pl.BlockSpec((bm, bk), lambda i, j, k: (i, k)), wspec],
        out_specs=pl.BlockSpec((bm, bn), lambda i, j, k: (i, j)),
        out_shape=jax.ShapeDtypeStruct((m, n), out_dtype),
        scratch_shapes=[pltpu.VMEM((bm, bn), F32)] if use_scratch else [],
        compiler_params=_params("parallel", "parallel", "arbitrary"),
        name="matmul",
    )(x, w)


def _swiglu_kernel(x_ref, wg_ref, wu_ref, o_ref):
    x = x_ref[...]
    o_ref[...] = (_silu(_dot(x, wg_ref[...])) * _dot(x, wu_ref[...])).astype(o_ref.dtype)


def _swiglu_up(x, wg, wu, layer):
    m, kdim = x.shape
    n = wg.shape[-1]
    bm = _pick(m, 1040, 16)
    bn = _pick(n, 512, LANES)
    wspec = pl.BlockSpec((None, kdim, bn), lambda i, j: (layer, 0, j))
    return pl.pallas_call(
        _swiglu_kernel,
        grid=(m // bm, n // bn),
        in_specs=[pl.BlockSpec((bm, kdim), lambda i, j: (i, 0)), wspec, wspec],
        out_specs=pl.BlockSpec((bm, bn), lambda i, j: (i, j)),
        out_shape=jax.ShapeDtypeStruct((m, n), BF16),
        compiler_params=_params("parallel", "parallel"),
        name="swiglu_up",
    )(x, wg, wu)


def _rope_pair(pe, cos, sin):
    return pe * cos + pltpu.roll(pe, LANES // 2, 1) * sin


def _qprep_kernel(cq_ref, g_ref, wq_ref, wk_ref, cos_ref, sin_ref, o_ref, *, nheads, nope, dlat, scale):
    cqn = _rms_rows(cq_ref[...], g_ref[...]).astype(BF16)
    q = _dot(cqn, wq_ref[...])
    cos, sin = cos_ref[...], sin_ref[...]
    pe0 = nheads * nope
    for h in range(nheads):
        qn = q[:, h * nope:(h + 1) * nope].astype(BF16)
        o_ref[h, :, 0:dlat] = (_dot(qn, wk_ref[h]) * scale).astype(o_ref.dtype)
        pe = q[:, pe0 + h * LANES:pe0 + (h + 1) * LANES]
        o_ref[h, :, dlat:dlat + LANES] = (_rope_pair(pe, cos, sin) * scale).astype(o_ref.dtype)


def _qprep(cq, g_q, wq, wk, cos, sin, scale):
    m, cdim = cq.shape
    nheads, nope, dlat = wk.shape
    tm = _pick(m, 256, 16)
    full = lambda a: pl.BlockSpec(a.shape, lambda i: (0,) * a.ndim)
    row = lambda a: pl.BlockSpec((tm, a.shape[1]), lambda i: (i, 0))
    return pl.pallas_call(
        functools.partial(_qprep_kernel, nheads=nheads, nope=nope, dlat=dlat, scale=scale),
        grid=(m // tm,),
        in_specs=[row(cq), full(g_q), full(wq), full(wk), row(cos), row(sin)],
        out_specs=pl.BlockSpec((nheads, tm, dlat + LANES), lambda i: (0, i, 0)),
        out_shape=jax.ShapeDtypeStruct((nheads, m, dlat + LANES), BF16),
        compiler_params=_params("parallel"),
        name="mla_qprep",
    )(cq, g_q, wq, wk, cos, sin)


def _kvprep_kernel(u_ref, g_ref, cos_ref, sin_ref, row_ref, k_ref, *, dlat, rope):
    u = u_ref[...]
    lat = _rms_rows(u[:, :dlat], g_ref[...])
    r = _rope_pair(u[:, dlat:dlat + LANES], cos_ref[...], sin_ref[...])
    row_ref[:, 0:dlat] = lat
    row_ref[:, dlat:dlat + rope] = r[:, :rope]
    k_ref[:, 0:dlat] = lat.astype(k_ref.dtype)
    k_ref[:, dlat:dlat + LANES] = r.astype(k_ref.dtype)


def _kvprep(u, g_kv, cos, sin, rope):
    m = u.shape[0]
    dlat = g_kv.shape[1]
    tm = _pick(m, 512, 16)
    row = lambda w: pl.BlockSpec((tm, w), lambda i: (i, 0))
    return pl.pallas_call(
        functools.partial(_kvprep_kernel, dlat=dlat, rope=rope),
        grid=(m // tm,),
        in_specs=[row(u.shape[1]), pl.BlockSpec((1, dlat), lambda i: (0, 0)), row(LANES), row(LANES)],
        out_specs=[row(dlat + rope), row(dlat + LANES)],
        out_shape=[jax.ShapeDtypeStruct((m, dlat + rope), F32), jax.ShapeDtypeStruct((m, dlat + LANES), BF16)],
        compiler_params=_params("parallel"),
        name="mla_kvprep",
    )(u, g_kv, cos, sin)


def _online_softmax_step(m_ref, l_ref, acc_ref, s, pv, axis=1):
    m_prev = m_ref[...]
    m_new = jnp.maximum(m_prev, jnp.max(s, axis=axis, keepdims=True))
    alpha = jnp.exp(m_prev - m_new)
    p = jnp.exp(s - m_new)
    l_ref[...] = alpha * l_ref[...] + jnp.sum(p, axis=axis, keepdims=True)
    acc_ref[...] = alpha * acc_ref[...] + pv(p.astype(BF16))
    m_ref[...] = m_new


def _causal_blocks(q_start, tq, tk, step):
    n_full = q_start // tk
    n_all = (q_start + tq + tk - 1) // tk

    def run(masked):
        def body(kb, carry):
            step(kb, masked)
            return carry
        return body

    lax.fori_loop(0, n_full, run(False), 0)
    lax.fori_loop(n_full, n_all, run(True), 0)


def _query_tokens(q_start, tq, cols):
    return q_start + lax.rem(lax.broadcasted_iota(jnp.int32, (1, cols), 1), tq)


def _mask_keys(st, off, tok, masked):
    if not masked:
        return st
    key = off + lax.broadcasted_iota(jnp.int32, (st.shape[0], 1), 0)
    return jnp.where(key <= tok, st, NEG)


def _key_blocks_t(k, tk):
    rows, d = k.shape
    return jnp.transpose(k.reshape(rows // tk, tk, d), (0, 2, 1))


def _flash_mla_kernel(q_ref, k_ref, kt_ref, wv_ref, o_ref, m_ref, l_ref, acc_ref, *, tq, tk, nheads, dlat, dv):
    cols = nheads * tq
    q = q_ref[...].reshape(cols, q_ref.shape[-1])
    q_start = pl.program_id(1) * tq
    tok = _query_tokens(q_start, tq, cols)
    m_ref[...] = jnp.full_like(m_ref, NEG)
    l_ref[...] = jnp.zeros_like(l_ref)
    acc_ref[...] = jnp.zeros_like(acc_ref)

    def step(kb, masked):
        off = pl.multiple_of(kb * tk, tk)
        st = _mask_keys(_dot_nt(k_ref[pl.ds(off, tk), :], q), off, tok, masked)
        vt = kt_ref[kb, 0:dlat, :]
        _online_softmax_step(m_ref, l_ref, acc_ref, st, lambda p: _dot(vt, p), axis=0)

    _causal_blocks(q_start, tq, tk, step)
    ot = acc_ref[...] / l_ref[...]
    for h in range(nheads):
        oth = ot[:, h * tq:(h + 1) * tq].astype(BF16)
        o_ref[:, h * dv:(h + 1) * dv] = _dot_tn(oth, wv_ref[h]).astype(o_ref.dtype)


def _flash_mla(q, k, wv, batch, seq):
    nheads, _, dq = q.shape
    _, dlat, dv = wv.shape
    tq = _pick(seq, FLASH_TQ, LANES)
    tk = _pick(seq, FLASH_TK, LANES)
    nq, nkb = seq // tq, seq // tk
    cols = nheads * tq
    kt = _key_blocks_t(k[:batch * seq], tk)
    return pl.pallas_call(
        functools.partial(_flash_mla_kernel, tq=tq, tk=tk, nheads=nheads, dlat=dlat, dv=dv),
        grid=(batch, nq),
        in_specs=[pl.BlockSpec((nheads, tq, dq), lambda b, i: (0, b * nq + i, 0)),
                  pl.BlockSpec((seq, dq), lambda b, i: (b, 0)),
                  pl.BlockSpec((nkb, dq, tk), lambda b, i: (b, 0, 0)),
                  pl.BlockSpec(wv.shape, lambda b, i: (0, 0, 0))],
        out_specs=pl.BlockSpec((tq, nheads * dv), lambda b, i: (b * nq + i, 0)),
        out_shape=jax.ShapeDtypeStruct((batch * seq, nheads * dv), BF16),
        scratch_shapes=[pltpu.VMEM((1, cols), F32), pltpu.VMEM((1, cols), F32), pltpu.VMEM((dlat, cols), F32)],
        compiler_params=_params("parallel", "parallel"),
        name="flash_mla",
    )(q, k, kt, wv)


def _diff_lambda(lam_ref, lam_init):
    lp = lam_ref[...]
    s01 = jnp.sum(lp[0:1] * lp[1:2], axis=1, keepdims=True)
    s23 = jnp.sum(lp[2:3] * lp[3:4], axis=1, keepdims=True)
    return jnp.exp(s01) - jnp.exp(s23) + lam_init


def _diff_combine(o1, o2, lam, g, lam_init):
    return _rms_rows(o1 - lam * o2, g) * (1.0 - lam_init)


def _flash_diff_kernel(q_ref, k_ref, vt_ref, lam_ref, g_ref, o_ref, m_ref, l_ref, acc_ref, *, tq, tk, nh, dd, lam_init):
    cols = nh * tq
    q = q_ref[...]
    qs = [jnp.concatenate([q[:, (2 * h + j) * dd:(2 * h + j + 1) * dd] for h in range(nh)], axis=0) for j in range(2)]
    q_start = pl.program_id(1) * tq
    tok = _query_tokens(q_start, tq, cols)
    m_ref[...] = jnp.full_like(m_ref, NEG)
    l_ref[...] = jnp.zeros_like(l_ref)
    acc_ref[...] = jnp.zeros_like(acc_ref)

    def step(kb, masked):
        off = pl.multiple_of(kb * tk, tk)
        k = k_ref[pl.ds(off, tk), :].astype(BF16)
        vt = vt_ref[kb]
        for j in range(2):
            st = _mask_keys(_dot_nt(k[:, j * dd:(j + 1) * dd], qs[j]), off, tok, masked)
            _online_softmax_step(m_ref.at[j], l_ref.at[j], acc_ref.at[j], st, lambda p: _dot(vt, p), axis=0)

    _causal_blocks(q_start, tq, tk, step)
    lam = _diff_lambda(lam_ref, lam_init)
    ot = acc_ref[0] / l_ref[0] - lam * (acc_ref[1] / l_ref[1])
    rt = ot * lax.rsqrt(jnp.mean(ot * ot, axis=0, keepdims=True) + NORM_EPS) * g_ref[...] * (1.0 - lam_init)
    for h in range(nh):
        o_ref[:, h * 2 * dd:(h + 1) * 2 * dd] = rt[:, h * tq:(h + 1) * tq].T.astype(o_ref.dtype)


def _flash_diff(q, k, v, lam_p, g_sub, batch, seq, lam_init):
    dd = k.shape[1] // 2
    nh = q.shape[1] // (2 * dd)
    tq = _pick(seq, FLASH_TQ, LANES)
    tk = _pick(seq, FLASH_TK, LANES)
    nq, nkb = seq // tq, seq // tk
    cols = nh * tq
    vt = _key_blocks_t(v[:batch * seq].astype(BF16), tk)
    g_col = g_sub.reshape(2 * dd, 1)
    return pl.pallas_call(
        functools.partial(_flash_diff_kernel, tq=tq, tk=tk, nh=nh, dd=dd, lam_init=lam_init),
        grid=(batch, nq),
        in_specs=[pl.BlockSpec((tq, q.shape[1]), lambda b, i: (b * nq + i, 0)),
                  pl.BlockSpec((seq, 2 * dd), lambda b, i: (b, 0)),
                  pl.BlockSpec((nkb, 2 * dd, tk), lambda b, i: (b, 0, 0)),
                  pl.BlockSpec(lam_p.shape, lambda b, i: (0, 0)),
                  pl.BlockSpec(g_col.shape, lambda b, i: (0, 0))],
        out_specs=pl.BlockSpec((tq, nh * 2 * dd), lambda b, i: (b * nq + i, 0)),
        out_shape=jax.ShapeDtypeStruct((batch * seq, nh * 2 * dd), BF16),
        scratch_shapes=[pltpu.VMEM((2, 1, cols), F32), pltpu.VMEM((2, 1, cols), F32), pltpu.VMEM((2, 2 * dd, cols), F32)],
        compiler_params=_params("parallel", "parallel"),
        name="flash_diff",
    )(q, k, vt, lam_p, g_col)


def _single_key_update(m_ref, l_ref, acc_ref, s_new, v_new):
    m_prev = m_ref[...]
    m_new = jnp.maximum(m_prev, s_new)
    alpha = jnp.exp(m_prev - m_new)
    p = jnp.exp(s_new - m_new)
    l = alpha * l_ref[...] + p
    acc = alpha * acc_ref[...] + p.astype(BF16).astype(F32) * v_new
    return acc / l


def _decode_attn_kernel(pt_ref, q_ref, dq_ref, kn_ref, dkn_ref, dvn_ref, lam_ref, g_ref, *rest,
                        pps, page, nh_d, dd, dlat, row, lam_init):
    del pt_ref
    mla_pages, dk_pages, dv_pages = rest[:pps], rest[pps:2 * pps], rest[2 * pps:3 * pps]
    olat_ref, od_ref = rest[3 * pps:3 * pps + 2]
    kbuf, m1, l1, a1, m2, l2, a2 = rest[3 * pps + 2:]
    c = pl.program_id(1)

    @pl.when(c == 0)
    def _():
        kbuf[...] = jnp.zeros_like(kbuf)
        for m_ref, l_ref, a_ref in ((m1, l1, a1), (m2, l2, a2)):
            m_ref[...] = jnp.full_like(m_ref, NEG)
            l_ref[...] = jnp.zeros_like(l_ref)
            a_ref[...] = jnp.zeros_like(a_ref)

    for r in range(pps):
        kbuf[0:row, r * page:(r + 1) * page] = mla_pages[r][...].astype(BF16)
    kt = kbuf[...]
    q = q_ref[...]
    _online_softmax_step(m1, l1, a1, _dot(q, kt), lambda p: _dot_nt(p, kt[0:dlat]))

    k1 = jnp.concatenate([p[pl.ds(0, page, stride=2), :] for p in dk_pages], axis=0).astype(BF16)
    k2 = jnp.concatenate([p[pl.ds(1, page, stride=2), :] for p in dk_pages], axis=0).astype(BF16)
    dv = jnp.concatenate([p[...] for p in dv_pages], axis=0).astype(BF16)
    dq = dq_ref[...]
    first = lax.broadcasted_iota(jnp.int32, (2 * nh_d, 1), 0) < nh_d
    s2 = jnp.where(first, _dot_nt(dq, k1), _dot_nt(dq, k2))
    _online_softmax_step(m2, l2, a2, s2, lambda p: _dot(p, dv))

    @pl.when(c == pl.num_programs(1) - 1)
    def _():
        kn = kn_ref[...].astype(F32)
        sn = jnp.sum(q.astype(F32) * kn, axis=1, keepdims=True)
        olat_ref[...] = _single_key_update(m1, l1, a1, sn, kn[:, :dlat])
        dkn = dkn_ref[...].astype(BF16).astype(F32)
        dvn = dvn_ref[...].astype(BF16).astype(F32)
        dqf = dq.astype(F32)
        sn2 = jnp.where(first, jnp.sum(dqf * dkn[:, :dd], axis=1, keepdims=True),
                        jnp.sum(dqf * dkn[:, dd:], axis=1, keepdims=True))
        on = _single_key_update(m2, l2, a2, sn2, dvn)
        lam = _diff_lambda(lam_ref, lam_init)
        od_ref[...] = _diff_combine(on[:nh_d], on[nh_d:], lam, g_ref[...], lam_init).astype(od_ref.dtype)


def _decode_attn(page_table, q, dq, k_new, dk_new, dv_new, lam_p, g_sub, cache_mla, cache_dk, cache_dv, lam_init):
    nb, n_pages = page_table.shape
    n_pool, page, row = cache_mla.shape
    nheads, dq_w = q.shape[1:]
    dd = cache_dv.shape[-1] // 2
    nh_d = dq.shape[1] // 2
    dlat = dq_w - LANES
    pps = _pick(n_pages, DECODE_PAGES, 1)
    cache_mla = jnp.transpose(cache_mla, (0, 2, 1))
    cache_dk = cache_dk.reshape(n_pool, 2 * page, dd)

    def page_spec(rows, width, r):
        return pl.BlockSpec((None, rows, width), lambda b, c, pt: (pt[b * n_pages + c * pps + r], 0, 0))

    per_seq = lambda a: pl.BlockSpec((None,) + a.shape[1:], lambda b, c, pt: (b, 0, 0))
    full = lambda a: pl.BlockSpec(a.shape, lambda b, c, pt: (0, 0))
    grid_spec = pltpu.PrefetchScalarGridSpec(
        num_scalar_prefetch=1,
        grid=(nb, n_pages // pps),
        in_specs=([per_seq(q), per_seq(dq), per_seq(k_new), per_seq(dk_new), per_seq(dv_new), full(lam_p), full(g_sub)]
                  + [page_spec(row, page, r) for r in range(pps)]
                  + [page_spec(2 * page, dd, r) for r in range(pps)]
                  + [page_spec(page, 2 * dd, r) for r in range(pps)]),
        out_specs=[pl.BlockSpec((None, nheads, dlat), lambda b, c, pt: (b, 0, 0)),
                   pl.BlockSpec((None, nh_d, 2 * dd), lambda b, c, pt: (b, 0, 0))],
        scratch_shapes=[pltpu.VMEM((dq_w, pps * page), BF16),
                        pltpu.VMEM((nheads, 1), F32), pltpu.VMEM((nheads, 1), F32), pltpu.VMEM((nheads, dlat), F32),
                        pltpu.VMEM((2 * nh_d, 1), F32), pltpu.VMEM((2 * nh_d, 1), F32), pltpu.VMEM((2 * nh_d, 2 * dd), F32)],
    )
    return pl.pallas_call(
        functools.partial(_decode_attn_kernel, pps=pps, page=page, nh_d=nh_d, dd=dd, dlat=dlat, row=row, lam_init=lam_init),
        grid_spec=grid_spec,
        out_shape=[jax.ShapeDtypeStruct((nb, nheads, dlat), F32), jax.ShapeDtypeStruct((nb, nh_d, 2 * dd), BF16)],
        compiler_params=_params("parallel", "arbitrary"),
        name="decode_attn",
    )(page_table.reshape(-1), q, dq, k_new, dk_new, dv_new, lam_p, g_sub,
      *([cache_mla] * pps), *([cache_dk] * pps), *([cache_dv] * pps))


def _head_proj_kernel(x_ref, w_ref, o_ref):
    o_ref[...] = _dot(x_ref[...].astype(BF16), w_ref[...]).astype(o_ref.dtype)


def _head_proj(x, w):
    nheads, n, c = x.shape
    v = w.shape[2]
    return pl.pallas_call(
        _head_proj_kernel,
        grid=(nheads,),
        in_specs=[pl.BlockSpec((None, n, c), lambda h: (h, 0, 0)), pl.BlockSpec((None, c, v), lambda h: (h, 0, 0))],
        out_specs=pl.BlockSpec((n, v), lambda h: (0, h)),
        out_shape=jax.ShapeDtypeStruct((n, nheads * v), BF16),
        compiler_params=_params("parallel"),
        name="head_proj",
    )(x, w)


def _softplus(x):
    return jnp.maximum(x, 0.0) + jnp.log1p(jnp.exp(-jnp.abs(x)))


def _dtprep_kernel(raw_ref, bias_ref, alog_ref, dt_ref, da_ref, cum_ref):
    dt = _softplus(raw_ref[...] + bias_ref[...])
    da = dt * -jnp.exp(alog_ref[...])
    dt_ref[...] = dt
    da_ref[...] = da
    rows = da.shape[0]
    row = lax.broadcasted_iota(jnp.int32, da.shape, 0)
    cum = da
    shift = 1
    while shift < rows:
        cum = cum + jnp.where(row >= shift, pltpu.roll(cum, shift, 0), 0.0)
        shift *= 2
    cum_ref[...] = cum


def _dtprep(raw, bias, a_log, chunk):
    m, nh = raw.shape
    spec = pl.BlockSpec((chunk, nh), lambda i: (i, 0))
    vec = pl.BlockSpec((1, nh), lambda i: (0, 0))
    shape = jax.ShapeDtypeStruct((m, nh), F32)
    return pl.pallas_call(
        _dtprep_kernel,
        grid=(m // chunk,),
        in_specs=[spec, vec, vec],
        out_specs=[spec, spec, spec],
        out_shape=[shape, shape, shape],
        compiler_params=_params("parallel"),
        name="ssd_dtprep",
    )(raw, bias.reshape(1, nh), a_log.reshape(1, nh))


def _conv_kernel(x_ref, halo_ref, w_ref, b_ref, o_ref, buf, *, tt, width, halo):
    j = pl.program_id(1)
    buf[0:halo, :] = jnp.where(j == 0, 0.0, halo_ref[...])
    buf[halo:halo + tt, :] = x_ref[...]
    acc = b_ref[...] + buf[pl.ds(halo - (width - 1), tt), :] * w_ref[0:1, :]
    for k in range(1, width):
        acc = acc + buf[pl.ds(halo - (width - 1) + k, tt), :] * w_ref[k:k + 1, :]
    o_ref[...] = _silu(acc)


def _conv_prompt(x, w, b, batch, seq):
    cdim = x.shape[1]
    width = w.shape[0]
    halo = 8
    tt = _pick(seq, 256, 8)
    tc = _pick(cdim, 2048, LANES)
    nt = seq // tt
    return pl.pallas_call(
        functools.partial(_conv_kernel, tt=tt, width=width, halo=halo),
        grid=(batch, nt, cdim // tc),
        in_specs=[pl.BlockSpec((tt, tc), lambda b_, j, c: (b_ * nt + j, c)),
                  pl.BlockSpec((halo, tc), lambda b_, j, c: (jnp.maximum((b_ * nt + j) * (tt // halo) - 1, 0), c)),
                  pl.BlockSpec((width, tc), lambda b_, j, c: (0, c)),
                  pl.BlockSpec((1, tc), lambda b_, j, c: (0, c))],
        out_specs=pl.BlockSpec((tt, tc), lambda b_, j, c: (b_ * nt + j, c)),
        out_shape=jax.ShapeDtypeStruct((batch * seq, cdim), F32),
        scratch_shapes=[pltpu.VMEM((halo + tt, tc), F32)],
        compiler_params=_params("parallel", "parallel", "parallel"),
        name="ssd_conv",
    )(x, x, w, b.reshape(1, cdim))


def _conv_decode_kernel(s_ref, x_ref, w_ref, b_ref, o_ref, *, width):
    acc = b_ref[...] + s_ref[0] * w_ref[0:1, :]
    for k in range(1, width - 1):
        acc = acc + s_ref[k] * w_ref[k:k + 1, :]
    o_ref[...] = _silu(acc + x_ref[...] * w_ref[width - 1:width, :])


def _conv_decode(state_t, x, w, b):
    n, cdim = x.shape
    width = w.shape[0]
    tc = _pick(cdim, 2048, LANES)
    return pl.pallas_call(
        functools.partial(_conv_decode_kernel, width=width),
        grid=(cdim // tc,),
        in_specs=[pl.BlockSpec((width - 1, n, tc), lambda c: (0, 0, c)),
                  pl.BlockSpec((n, tc), lambda c: (0, c)),
                  pl.BlockSpec((width, tc), lambda c: (0, c)),
                  pl.BlockSpec((1, tc), lambda c: (0, c))],
        out_specs=pl.BlockSpec((n, tc), lambda c: (0, c)),
        out_shape=jax.ShapeDtypeStruct((n, cdim), F32),
        compiler_params=_params("parallel"),
        name="ssd_conv_decode",
    )(state_t, x, w, b.reshape(1, cdim))


def _gate_norm(y, z, g):
    return _rms_rows(y * _silu(z), g)


def _ssd_chunk_kernel(x_ref, b_ref, c_ref, z_ref, cum_ref, cumt_ref, dt_ref, dtt_ref, dskip_ref, g_ref,
                      y_ref, hout_ref, h_ref, y_scr, *, chunk, hpg, hd):
    ci = pl.program_id(2)

    @pl.when(ci == 0)
    def _():
        h_ref[...] = jnp.zeros_like(h_ref)

    x = x_ref[...]
    bmat = b_ref[...]
    cmat = c_ref[...].astype(BF16)
    cb = _dot_nt(cmat, bmat.astype(BF16))
    h_all = h_ref[...]
    y_inter = _dot_nt(cmat, h_all.astype(BF16))
    cum = cum_ref[...]
    dt = dt_ref[...]
    causal = (lax.broadcasted_iota(jnp.int32, (chunk, chunk), 0) >= lax.broadcasted_iota(jnp.int32, (chunk, chunk), 1))
    for r in range(hpg):
        sl = slice(r * hd, (r + 1) * hd)
        xr = x[:, sl].astype(BF16)
        ac = cum[:, r:r + 1]
        seg = ac - cumt_ref[r:r + 1, :]
        wgt = cb * jnp.exp(jnp.where(causal, seg, -jnp.inf)) * dtt_ref[r:r + 1, :]
        y_scr[:, sl] = _dot(wgt.astype(BF16), xr) + y_inter[:, sl] * jnp.exp(ac)
        last = ac[chunk - 1:chunk, :]
        to_end = jnp.exp(last - ac) * dt[:, r:r + 1]
        h_ref[sl, :] = h_all[sl, :] * jnp.exp(last) + _dot_tn(xr, (bmat * to_end).astype(BF16))
    y = y_scr[...] + dskip_ref[...] * x
    y_ref[...] = _gate_norm(y, z_ref[...], g_ref[...]).astype(y_ref.dtype)

    @pl.when(ci == pl.num_programs(2) - 1)
    def _():
        hout_ref[...] = h_ref[...]


def _ssd_prompt(xbc, z, cum_g, cum_t, dt_g, dt_t, dskip, g_norm, batch, seq, chunk, ngroups, nstate):
    d_inner = z.shape[1]
    gw = d_inner // ngroups
    hpg = cum_g.shape[2]
    hd = gw // hpg
    nc = seq // chunk
    xoff = d_inner // nstate
    rows = lambda b, g, c: b * nc + c
    return pl.pallas_call(
        functools.partial(_ssd_chunk_kernel, chunk=chunk, hpg=hpg, hd=hd),
        grid=(batch, ngroups, nc),
        in_specs=[pl.BlockSpec((chunk, gw), lambda b, g, c: (rows(b, g, c), g)),
                  pl.BlockSpec((chunk, nstate), lambda b, g, c: (rows(b, g, c), xoff + g)),
                  pl.BlockSpec((chunk, nstate), lambda b, g, c: (rows(b, g, c), xoff + ngroups + g)),
                  pl.BlockSpec((chunk, gw), lambda b, g, c: (rows(b, g, c), g)),
                  pl.BlockSpec((None, chunk, hpg), lambda b, g, c: (g, rows(b, g, c), 0)),
                  pl.BlockSpec((hpg, chunk), lambda b, g, c: (g, rows(b, g, c))),
                  pl.BlockSpec((None, chunk, hpg), lambda b, g, c: (g, rows(b, g, c), 0)),
                  pl.BlockSpec((hpg, chunk), lambda b, g, c: (g, rows(b, g, c))),
                  pl.BlockSpec((1, gw), lambda b, g, c: (0, g)),
                  pl.BlockSpec((1, gw), lambda b, g, c: (0, g))],
        out_specs=[pl.BlockSpec((chunk, gw), lambda b, g, c: (rows(b, g, c), g)),
                   pl.BlockSpec((None, gw, nstate), lambda b, g, c: (b * ngroups + g, 0, 0))],
        out_shape=[jax.ShapeDtypeStruct((batch * seq, d_inner), BF16),
                   jax.ShapeDtypeStruct((batch * ngroups, gw, nstate), F32)],
        scratch_shapes=[pltpu.VMEM((gw, nstate), F32), pltpu.VMEM((chunk, gw), F32)],
        compiler_params=_params("parallel", "parallel", "arbitrary"),
        name="ssd_chunk_scan",
    )(xbc, xbc, xbc, z, cum_g, cum_t, dt_g, dt_t, dskip, g_norm)


def _split3(a):
    hi = a.astype(BF16)
    r1 = a - hi.astype(F32)
    mid = r1.astype(BF16)
    lo = (r1 - mid.astype(F32)).astype(BF16)
    return jnp.concatenate([hi, mid, lo], axis=0)


def _ssd_step_kernel(h_ref, x_ref, b_ref, c_ref, z_ref, dt_ref, da_ref, dskip_ref, g_ref, hout_ref, y_ref, y_scr,
                     *, sb, nstate):
    gw = x_ref.shape[1]
    x = x_ref[...]
    xdt = x * dt_ref[...]
    decay = jnp.exp(da_ref[...])
    prow = lax.broadcasted_iota(jnp.int32, (16, gw), 0)
    srow = lax.broadcasted_iota(jnp.int32, (48, 2 * nstate), 0) % 16
    scol = lax.broadcasted_iota(jnp.int32, (48, 2 * nstate), 1)
    sel = jnp.where(((srow == 0) & (scol < nstate)) | ((srow == 1) & (scol >= nstate)), 1.0, 0.0).astype(BF16)
    for s in range(sb):
        pair = jnp.where(prow == 0, xdt[s:s + 1, :], jnp.where(prow == 1, decay[s:s + 1, :], 0.0))
        bc = _dot_tn(_split3(pair), sel)
        hn = h_ref[s].reshape(gw, nstate) * bc[:, nstate:] + bc[:, :nstate] * b_ref[s:s + 1, :]
        hout_ref[s] = hn.reshape(hout_ref.shape[1:])
        crow = jnp.broadcast_to(c_ref[s:s + 1, :], (16, nstate)).astype(BF16)
        y_scr[s:s + 1, :] = _dot_nt(crow, hn.astype(BF16))[0:1]
    y = y_scr[...] + dskip_ref[...] * x
    y_ref[...] = _gate_norm(y, z_ref[...], g_ref[...]).astype(y_ref.dtype)


def _ssd_step(h, act, z, z_row0, dt_rep, da_rep, dskip, g_norm, ngroups):
    n, nheads, hd, nstate = h.shape
    d_inner = nheads * hd
    hpg = nheads // ngroups
    gw = hpg * hd
    sb = _pick(n, 8, 8)
    assert z_row0 % sb == 0
    xoff = d_inner // nstate
    hspec = pl.BlockSpec((sb, hpg, hd, nstate), lambda i, g: (i, g, 0, 0))
    chan = pl.BlockSpec((sb, gw), lambda i, g: (i, g))
    vec = pl.BlockSpec((1, gw), lambda i, g: (0, g))
    return pl.pallas_call(
        functools.partial(_ssd_step_kernel, sb=sb, nstate=nstate),
        grid=(n // sb, ngroups),
        in_specs=[hspec, chan,
                  pl.BlockSpec((sb, nstate), lambda i, g: (i, xoff + g)),
                  pl.BlockSpec((sb, nstate), lambda i, g: (i, xoff + ngroups + g)),
                  pl.BlockSpec((sb, gw), lambda i, g: (z_row0 // sb + i, g)),
                  chan, chan, vec, vec],
        out_specs=[hspec, chan],
        out_shape=[jax.ShapeDtypeStruct(h.shape, F32), jax.ShapeDtypeStruct((n, d_inner), F32)],
        scratch_shapes=[pltpu.VMEM((sb, gw), F32)],
        compiler_params=_params("parallel", "parallel"),
        name="ssd_step",
    )(h, act, act, act, z, dt_rep, da_rep, dskip, g_norm)


def _rot_cols(w):
    half = w.shape[-1] // 2
    return jnp.concatenate([-w[..., half:], w[..., :half]], axis=-1)


def _attn_layer(h, layer, mp, seq, batch, page_table, caches, cos, sin, w_in, g_q, g_kv, w_qb, w_kvb, lam_p, g_sub, w_out):
    cache_mla, cache_dk, cache_dv = caches
    q_lora = g_q.shape[0]
    dlat = g_kv.shape[0]
    rope = cache_mla.shape[-1] - dlat
    dd = cache_dv.shape[-1] // 2
    nheads = w_qb.shape[1]
    nope = w_qb.shape[2] - rope
    dv = w_kvb.shape[2] - nope
    assert 2 * rope == LANES, "the rope rotation is done inside one 128-lane chunk"
    lam_init = 0.8 - 0.6 * math.exp(-0.3 * layer)
    nd = h.shape[0] - mp

    o_ckv, o_kpe, o_dq = q_lora, q_lora + dlat, q_lora + dlat + rope
    o_dk = w_in.shape[1] - 4 * dd
    w_cq = w_in[:, :q_lora].astype(BF16)
    w_kpe = w_in[:, o_kpe:o_dq]
    w_ckr = jnp.concatenate([w_in[:, o_ckv:o_kpe], w_kpe, _rot_cols(w_kpe)], axis=1).astype(BF16)
    w_dq = w_in[:, o_dq:o_dk].astype(BF16)
    w_dk = w_in[:, o_dk:o_dk + 2 * dd].astype(BF16)
    w_dv = w_in[:, o_dk + 2 * dd:].astype(BF16)
    w_pe = w_qb[..., nope:]
    wq = jnp.concatenate([w_qb[..., :nope].reshape(q_lora, nheads * nope),
                          jnp.concatenate([w_pe, _rot_cols(w_pe)], axis=-1).reshape(q_lora, nheads * LANES)],
                         axis=1).astype(BF16)
    wk = jnp.transpose(w_kvb[..., :nope], (1, 2, 0)).astype(BF16)
    wv = jnp.transpose(w_kvb[..., nope:], (1, 0, 2)).astype(BF16)

    cq = _matmul(h, w_cq)
    ckr = _matmul(h, w_ckr)
    dq = _matmul(h, w_dq, out_dtype=BF16, scale=dd ** -0.5)
    dk = _matmul(h, w_dk)
    dv_rows = _matmul(h, w_dv)

    q = _qprep(cq, g_q.reshape(1, -1), wq, wk, cos, sin, (nope + rope) ** -0.5)
    mla_row, k_pad = _kvprep(ckr, g_kv.reshape(1, -1), cos, sin, rope)

    g_sub2 = g_sub.reshape(1, -1)
    o_mla_p = _flash_mla(q, k_pad, wv, batch, seq)
    o_d_p = _flash_diff(dq, dk, dv_rows, lam_p, g_sub2, batch, seq, lam_init)

    q_s = jnp.transpose(q[:, mp:], (1, 0, 2))
    nh_d = dq.shape[1] // (2 * dd)
    dq_s = jnp.transpose(dq[mp:].reshape(nd, nh_d, 2, dd), (0, 2, 1, 3)).reshape(nd, 2 * nh_d, dd)
    o_lat_s, o_d_s = _decode_attn(page_table, q_s, dq_s, k_pad[mp:, None], dk[mp:, None], dv_rows[mp:, None],
                                  lam_p, g_sub2, cache_mla, cache_dk, cache_dv, lam_init)
    o_mla_s = _head_proj(jnp.transpose(o_lat_s, (1, 0, 2)), wv)

    o = jnp.concatenate([jnp.concatenate([o_mla_p, o_d_p], axis=1),
                         jnp.concatenate([o_mla_s, o_d_s.reshape(nd, -1)], axis=1)], axis=0)
    out = _matmul(o, w_out, layer=layer // 2)
    new = (mla_row[:mp].reshape(batch, seq, -1), dk[:mp].reshape(batch, seq, 2, dd), dv_rows[:mp].reshape(batch, seq, -1),
           mla_row[mp:].reshape(nd, 1, -1), dk[mp:].reshape(nd, 1, 2, dd), dv_rows[mp:].reshape(nd, 1, -1))
    return out, new


def _ssd_layer(h, j, mp, seq, batch, conv_state, ssm_state, w_in, conv_w, conv_b, dt_bias, a_log, d_skip, g_norm, w_out):
    nd, nheads, hd, nstate = ssm_state.shape
    d_inner = nheads * hd
    cdim = conv_w.shape[1]
    ngroups = (cdim - d_inner) // (2 * nstate)
    hpg = nheads // ngroups
    chunk = _pick(seq, 256, 8)
    width = conv_w.shape[0]

    z = _matmul(h, w_in, layer=j, cols=(0, d_inner))
    xbc = _matmul(h, w_in, layer=j, cols=(d_inner, cdim))
    dt_raw = _matmul(h, w_in, layer=j, cols=(d_inner + cdim, nheads))

    dskip = jnp.repeat(d_skip, hd).reshape(1, d_inner)
    g2 = g_norm.reshape(1, d_inner)

    dt_p, _, cum_p = _dtprep(dt_raw[:mp], dt_bias, a_log, chunk)
    by_group = lambda a: jnp.transpose(a.reshape(-1, ngroups, hpg), (1, 0, 2))
    xbc_act = _conv_prompt(xbc, conv_w, conv_b, batch, seq)
    y_p, h_p = _ssd_prompt(xbc_act, z, by_group(cum_p), cum_p.T, by_group(dt_p), dt_p.T, dskip, g2,
                           batch, seq, chunk, ngroups, nstate)
    conv_p = jnp.stack([xbc[b * seq + seq - (width - 1):(b + 1) * seq] for b in range(batch)])
    ssm_p = h_p.reshape(batch, nheads, hd, nstate)

    xbc_s = xbc[mp:]
    dt_s, da_s, _ = _dtprep(dt_raw[mp:], dt_bias, a_log, nd)
    act_s = _conv_decode(jnp.transpose(conv_state, (1, 0, 2)), xbc_s, conv_w, conv_b)
    ssm_s, y_s = _ssd_step(ssm_state, act_s, z, mp, jnp.repeat(dt_s, hd, axis=1), jnp.repeat(da_s, hd, axis=1),
                           dskip, g2, ngroups)
    conv_s = jnp.concatenate([conv_state[:, 1:], xbc_s[:, None]], axis=1)

    out = _matmul(jnp.concatenate([y_p, y_s.astype(BF16)], axis=0), w_out, layer=j)
    return out, (conv_p, ssm_p, conv_s, ssm_s)


def kernel(x_prompt, x_sample, cache_mla_l0, cache_dk_l0, cache_dv_l0, state_conv_l1, state_ssm_l1, cache_mla_l2, cache_dk_l2, cache_dv_l2, state_conv_l3, state_ssm_l3, page_table, norm_mix_pre, norm_mix_post, norm_ffn_pre, norm_ffn_post, ffn_w_gate, ffn_w_up, ffn_w_down, attn_w_in, mla_g_q, mla_g_kv, mla_w_qb, mla_w_kvb, diff_lambda, diff_g_sub, attn_w_out, ssm_w_in, ssm_conv_w, ssm_conv_b, ssm_dt_bias, ssm_a_log, ssm_d, ssm_g_norm, ssm_w_out):
    batch, seq, d_model = x_prompt.shape
    nd, dec_seq, _ = x_sample.shape
    assert dec_seq == 1, "decode rows are one new token per sequence"
    mp = batch * seq
    depth = norm_mix_pre.shape[0]
    attn_caches = {0: (cache_mla_l0, cache_dk_l0, cache_dv_l0), 2: (cache_mla_l2, cache_dk_l2, cache_dv_l2)}
    ssm_states = {1: (state_conv_l1, state_ssm_l1), 3: (state_conv_l3, state_ssm_l3)}

    rope = cache_mla_l0.shape[-1] - mla_g_kv.shape[1]
    half = rope // 2
    past_len = page_table.shape[1] * cache_mla_l0.shape[1]
    pos = jnp.concatenate([jnp.tile(jnp.arange(seq), batch), jnp.full((nd,), past_len)])
    inv = jnp.exp(-math.log(ROPE_BASE) * jnp.arange(half, dtype=F32) / half)
    ang = pos.astype(F32)[:, None] * inv[None, :]
    pad = jnp.zeros((mp + nd, LANES - rope), F32)
    cos = jnp.concatenate([jnp.cos(ang), jnp.cos(ang), pad], axis=1)
    sin = jnp.concatenate([jnp.sin(ang), jnp.sin(ang), pad], axis=1)

    d_ff = ffn_w_gate.shape[2]
    ff_pad = -d_ff % FFN_ALIGN
    wg = jnp.pad(ffn_w_gate.astype(BF16), ((0, 0), (0, 0), (0, ff_pad)))
    wu = jnp.pad(ffn_w_up.astype(BF16), ((0, 0), (0, 0), (0, ff_pad)))
    wd = jnp.pad(ffn_w_down.astype(BF16), ((0, 0), (0, ff_pad), (0, 0)))
    w_attn_out = attn_w_out.astype(BF16)
    w_ssm_in = ssm_w_in.astype(BF16)
    w_ssm_out = ssm_w_out.astype(BF16)

    x = jnp.concatenate([x_prompt.reshape(mp, d_model), x_sample.reshape(nd, d_model)], axis=0)
    h = _rms(x, norm_mix_pre[0])
    new = {}
    for layer in range(depth):
        j = layer // 2
        if layer % 2 == 0:
            o, new[layer] = _attn_layer(h, layer, mp, seq, batch, page_table, attn_caches[layer], cos, sin,
                                        attn_w_in[j], mla_g_q[j], mla_g_kv[j], mla_w_qb[j], mla_w_kvb[j],
                                        diff_lambda[j], diff_g_sub[j], w_attn_out)
        else:
            o, new[layer] = _ssd_layer(h, j, mp, seq, batch, *ssm_states[layer], w_ssm_in, ssm_conv_w[j], ssm_conv_b[j],
                                       ssm_dt_bias[j], ssm_a_log[j], ssm_d[j], ssm_g_norm[j], w_ssm_out)
        x, h = _add_rms(x, o, norm_mix_post[layer], norm_ffn_pre[layer])
        o = _matmul(_swiglu_up(h, wg, wu, layer), wd, layer=layer, bk=2816)
        x, h = _add_rms(x, o, norm_ffn_post[layer], norm_mix_pre[layer + 1] if layer + 1 < depth else None)
    outs = [x[:mp].reshape(batch, seq, d_model), x[mp:].reshape(nd, 1, d_model)]
    for layer in range(depth):
        outs.extend(new[layer])
    return tuple(outs)
```

```python
import functools
import math

import jax
import jax.numpy as jnp
from jax import lax
from jax.experimental import pallas as pl
from jax.experimental.pallas import tpu as pltpu

F32 = jnp.float32
BF16 = jnp.bfloat16
NORM_EPS = 1e-6
ROPE_BASE = 10000.0
NEG = -1e30
LANES = 128
VMEM_LIMIT = 56 * 1024 * 1024
FFN_ALIGN = 1024
FLASH_TQ = 128
FLASH_TK = 512
DECODE_PAGES = 32


def _pick(n, target, align):
    if n <= target:
        return n
    d = target - target % align
    while d >= align:
        if n % d == 0:
            return d
        d -= align
    return n


def _params(*sem):
    return pltpu.CompilerParams(dimension_semantics=sem, vmem_limit_bytes=VMEM_LIMIT)


def _rms_rows(x, g):
    return x * lax.rsqrt(jnp.mean(x * x, axis=-1, keepdims=True) + NORM_EPS) * g


def _silu(x):
    return x * jax.nn.sigmoid(x)


def _dot(a, b):
    return jnp.dot(a, b, preferred_element_type=F32)


def _dot_nt(a, b):
    return lax.dot_general(a, b, (((1,), (1,)), ((), ())), preferred_element_type=F32)


def _dot_tn(a, b):
    return lax.dot_general(a, b, (((0,), (0,)), ((), ())), preferred_element_type=F32)


def _rms_kernel(x_ref, g_ref, o_ref):
    o_ref[...] = _rms_rows(x_ref[...], g_ref[...]).astype(o_ref.dtype)


def _rms(x, g, out_dtype=BF16):
    m, d = x.shape
    tm = _pick(m, 512, 16)
    return pl.pallas_call(
        _rms_kernel,
        grid=(m // tm,),
        in_specs=[pl.BlockSpec((tm, d), lambda i: (i, 0)), pl.BlockSpec((1, d), lambda i: (0, 0))],
        out_specs=pl.BlockSpec((tm, d), lambda i: (i, 0)),
        out_shape=jax.ShapeDtypeStruct((m, d), out_dtype),
        compiler_params=_params("parallel"),
        name="rms",
    )(x, g.reshape(1, d))


def _add_rms_kernel(x_ref, o_ref, gpost_ref, *rest, with_next):
    xn = x_ref[...] + _rms_rows(o_ref[...], gpost_ref[...])
    if with_next:
        gnext_ref, xo_ref, h_ref = rest
        h_ref[...] = _rms_rows(xn, gnext_ref[...]).astype(h_ref.dtype)
    else:
        (xo_ref,) = rest
    xo_ref[...] = xn


def _add_rms(x, o, g_post, g_next=None):
    m, d = x.shape
    tm = _pick(m, 256, 16)
    row = pl.BlockSpec((tm, d), lambda i: (i, 0))
    vec = pl.BlockSpec((1, d), lambda i: (0, 0))
    with_next = g_next is not None
    in_specs = [row, row, vec] + ([vec] if with_next else [])
    args = [x, o, g_post.reshape(1, d)] + ([g_next.reshape(1, d)] if with_next else [])
    out_shape = [jax.ShapeDtypeStruct((m, d), F32)] + ([jax.ShapeDtypeStruct((m, d), BF16)] if with_next else [])
    out = pl.pallas_call(
        functools.partial(_add_rms_kernel, with_next=with_next),
        grid=(m // tm,),
        in_specs=in_specs,
        out_specs=[row] * len(out_shape),
        out_shape=out_shape,
        compiler_params=_params("parallel"),
        name="add_rms",
    )(*args)
    return (out[0], out[1]) if with_next else (out[0], None)


def _mm_kernel(x_ref, w_ref, o_ref, *scratch, nk, scale):
    def finish(acc):
        return (acc * scale if scale != 1.0 else acc).astype(o_ref.dtype)

    if nk == 1:
        o_ref[...] = finish(_dot(x_ref[...], w_ref[...]))
        return
    acc_ref = scratch[0] if scratch else o_ref
    k = pl.program_id(2)

    @pl.when(k == 0)
    def _():
        acc_ref[...] = jnp.zeros_like(acc_ref)

    acc_ref[...] += _dot(x_ref[...], w_ref[...])
    if scratch or scale != 1.0:
        @pl.when(k == nk - 1)
        def _():
            o_ref[...] = finish(acc_ref[...])


def _matmul(x, w, out_dtype=F32, scale=1.0, layer=None, cols=None, bm=1040, bn=1024, bk=4096):
    m, kdim = x.shape
    n0, n = cols if cols is not None else (0, w.shape[-1])
    bm = _pick(m, bm, 16)
    bn = _pick(math.gcd(n, n0) if n0 else n, bn, LANES)
    bk = _pick(kdim, bk, LANES)
    nk = kdim // bk
    j0 = n0 // bn
    use_scratch = nk > 1 and out_dtype != F32
    if w.ndim == 3:
        wspec = pl.BlockSpec((None, bk, bn), lambda i, j, k: (layer, k, j0 + j))
    else:
        wspec = pl.BlockSpec((bk, bn), lambda i, j, k: (k, j0 + j))
    return pl.pallas_call(
        functools.partial(_mm_kernel, nk=nk, scale=scale),
        grid=(m // bm, n // bn, nk),
        in_specs=[pl.BlockSpec((bm, bk), lambda i, j, k: (i, k)), wspec],
        out_specs=pl.BlockSpec((bm, bn), lambda i, j, k: (i, j)),
        out_shape=jax.ShapeDtypeStruct((m, n), out_dtype),
        scratch_shapes=[pltpu.VMEM((bm, bn), F32)] if use_scratch else [],
        compiler_params=_params("parallel", "parallel", "arbitrary"),
        name="matmul",
    )(x, w)


def _cast_pad_cols_kernel(w_ref, o_ref):
    n = w_ref.shape[1]
    o_ref[:, 0:n] = w_ref[...].astype(o_ref.dtype)
    if o_ref.shape[1] > n:
        o_ref[:, n:] = jnp.zeros((o_ref.shape[0], o_ref.shape[1] - n), o_ref.dtype)


def _cast_pad_cols(w, pad):
    nl, kdim, n = w.shape
    tr = _pick(kdim, 256, 16)
    return pl.pallas_call(
        _cast_pad_cols_kernel,
        grid=(nl, kdim // tr),
        in_specs=[pl.BlockSpec((None, tr, n), lambda l, i: (l, i, 0))],
        out_specs=pl.BlockSpec((None, tr, n + pad), lambda l, i: (l, i, 0)),
        out_shape=jax.ShapeDtypeStruct((nl, kdim, n + pad), BF16),
        compiler_params=_params("parallel", "parallel"),
        name="cast_pad_cols",
    )(w)


def _cast_pad_rows_kernel(w_ref, o_ref, *, n_valid):
    keep = pl.program_id(1) < n_valid
    o_ref[...] = jnp.where(keep, w_ref[...], 0.0).astype(o_ref.dtype)


def _cast_pad_rows(w, pad):
    nl, kdim, n = w.shape
    tr = math.gcd(kdim, pad) if pad else _pick(kdim, 256, 16)
    n_valid = kdim // tr
    return pl.pallas_call(
        functools.partial(_cast_pad_rows_kernel, n_valid=n_valid),
        grid=(nl, (kdim + pad) // tr),
        in_specs=[pl.BlockSpec((None, tr, n), lambda l, i: (l, jnp.minimum(i, n_valid - 1), 0))],
        out_specs=pl.BlockSpec((None, tr, n), lambda l, i: (l, i, 0)),
        out_shape=jax.ShapeDtypeStruct((nl, kdim + pad, n), BF16),
        compiler_params=_params("parallel", "parallel"),
        name="cast_pad_rows",
    )(w)


def _swiglu_kernel(x_ref, wg_ref, wu_ref, o_ref):
    x = x_ref[...]
    o_ref[...] = (_silu(_dot(x, wg_ref[...])) * _dot(x, wu_ref[...])).astype(o_ref.dtype)


def _swiglu_up(x, wg, wu, layer):
    m, kdim = x.shape
    n = wg.shape[-1]
    bm = _pick(m, 1040, 16)
    bn = _pick(n, 512, LANES)
    wspec = pl.BlockSpec((None, kdim, bn), lambda i, j: (layer, 0, j))
    return pl.pallas_call(
        _swiglu_kernel,
        grid=(m // bm, n // bn),
        in_specs=[pl.BlockSpec((bm, kdim), lambda i, j: (i, 0)), wspec, wspec],
        out_specs=pl.BlockSpec((bm, bn), lambda i, j: (i, j)),
        out_shape=jax.ShapeDtypeStruct((m, n), BF16),
        compiler_params=_params("parallel", "parallel"),
        name="swiglu_up",
    )(x, wg, wu)


def _rope_pair(pe, cos, sin):
    return pe * cos + pltpu.roll(pe, LANES // 2, 1) * sin


def _qprep_kernel(cq_ref, g_ref, wq_ref, wk_ref, cos_ref, sin_ref, o_ref, *, nheads, nope, dlat, scale):
    cqn = _rms_rows(cq_ref[...], g_ref[...]).astype(BF16)
    q = _dot(cqn, wq_ref[...])
    cos, sin = cos_ref[...], sin_ref[...]
    pe0 = nheads * nope
    for h in range(nheads):
        qn = q[:, h * nope:(h + 1) * nope].astype(BF16)
        o_ref[h, :, 0:dlat] = (_dot(qn, wk_ref[h]) * scale).astype(o_ref.dtype)
        pe = q[:, pe0 + h * LANES:pe0 + (h + 1) * LANES]
        o_ref[h, :, dlat:dlat + LANES] = (_rope_pair(pe, cos, sin) * scale).astype(o_ref.dtype)


def _qprep(cq, g_q, wq, wk, cos, sin, scale):
    m, cdim = cq.shape
    nheads, nope, dlat = wk.shape
    tm = _pick(m, 256, 16)
    full = lambda a: pl.BlockSpec(a.shape, lambda i: (0,) * a.ndim)
    row = lambda a: pl.BlockSpec((tm, a.shape[1]), lambda i: (i, 0))
    return pl.pallas_call(
        functools.partial(_qprep_kernel, nheads=nheads, nope=nope, dlat=dlat, scale=scale),
        grid=(m // tm,),
        in_specs=[row(cq), full(g_q), full(wq), full(wk), row(cos), row(sin)],
        out_specs=pl.BlockSpec((nheads, tm, dlat + LANES), lambda i: (0, i, 0)),
        out_shape=jax.ShapeDtypeStruct((nheads, m, dlat + LANES), BF16),
        compiler_params=_params("parallel"),
        name="mla_qprep",
    )(cq, g_q, wq, wk, cos, sin)


def _kvprep_kernel(u_ref, g_ref, cos_ref, sin_ref, row_ref, k_ref, *, dlat, rope):
    u = u_ref[...]
    lat = _rms_rows(u[:, :dlat], g_ref[...])
    r = _rope_pair(u[:, dlat:dlat + LANES], cos_ref[...], sin_ref[...])
    row_ref[:, 0:dlat] = lat
    row_ref[:, dlat:dlat + rope] = r[:, :rope]
    k_ref[:, 0:dlat] = lat.astype(k_ref.dtype)
    k_ref[:, dlat:dlat + LANES] = r.astype(k_ref.dtype)


def _kvprep(u, g_kv, cos, sin, rope):
    m = u.shape[0]
    dlat = g_kv.shape[1]
    tm = _pick(m, 512, 16)
    row = lambda w: pl.BlockSpec((tm, w), lambda i: (i, 0))
    return pl.pallas_call(
        functools.partial(_kvprep_kernel, dlat=dlat, rope=rope),
        grid=(m // tm,),
        in_specs=[row(u.shape[1]), pl.BlockSpec((1, dlat), lambda i: (0, 0)), row(LANES), row(LANES)],
        out_specs=[row(dlat + rope), row(dlat + LANES)],
        out_shape=[jax.ShapeDtypeStruct((m, dlat + rope), F32), jax.ShapeDtypeStruct((m, dlat + LANES), BF16)],
        compiler_params=_params("parallel"),
        name="mla_kvprep",
    )(u, g_kv, cos, sin)


def _online_softmax_step(m_ref, l_ref, acc_ref, s, pv, axis=1):
    m_prev = m_ref[...]
    m_new = jnp.maximum(m_prev, jnp.max(s, axis=axis, keepdims=True))
    alpha = jnp.exp(m_prev - m_new)
    p = jnp.exp(s - m_new)
    l_ref[...] = alpha * l_ref[...] + jnp.sum(p, axis=axis, keepdims=True)
    acc_ref[...] = alpha * acc_ref[...] + pv(p.astype(BF16))
    m_ref[...] = m_new


def _causal_blocks(q_start, tq, tk, step):
    n_full = q_start // tk
    n_all = (q_start + tq + tk - 1) // tk

    def run(masked):
        def body(kb, carry):
            step(kb, masked)
            return carry
        return body

    lax.fori_loop(0, n_full, run(False), 0)
    lax.fori_loop(n_full, n_all, run(True), 0)


def _query_tokens(q_start, tq, cols):
    return q_start + lax.rem(lax.broadcasted_iota(jnp.int32, (1, cols), 1), tq)


def _mask_keys(st, off, tok, masked):
    if not masked:
        return st
    key = off + lax.broadcasted_iota(jnp.int32, (st.shape[0], 1), 0)
    return jnp.where(key <= tok, st, NEG)


def _key_blocks_t(k, tk):
    rows, d = k.shape
    return jnp.transpose(k.reshape(rows // tk, tk, d), (0, 2, 1))


def _flash_mla_kernel(q_ref, k_ref, kt_ref, wv_ref, o_ref, m_ref, l_ref, acc_ref, *, tq, tk, nheads, dlat, dv):
    cols = nheads * tq
    q = q_ref[...].reshape(cols, q_ref.shape[-1])
    q_start = pl.program_id(1) * tq
    tok = _query_tokens(q_start, tq, cols)
    m_ref[...] = jnp.full_like(m_ref, NEG)
    l_ref[...] = jnp.zeros_like(l_ref)
    acc_ref[...] = jnp.zeros_like(acc_ref)

    def step(kb, masked):
        off = pl.multiple_of(kb * tk, tk)
        st = _mask_keys(_dot_nt(k_ref[pl.ds(off, tk), :], q), off, tok, masked)
        vt = kt_ref[kb, 0:dlat, :]
        _online_softmax_step(m_ref, l_ref, acc_ref, st, lambda p: _dot(vt, p), axis=0)

    _causal_blocks(q_start, tq, tk, step)
    ot = acc_ref[...] / l_ref[...]
    for h in range(nheads):
        oth = ot[:, h * tq:(h + 1) * tq].astype(BF16)
        o_ref[:, h * dv:(h + 1) * dv] = _dot_tn(oth, wv_ref[h]).astype(o_ref.dtype)


def _flash_mla(q, k, wv, batch, seq):
    nheads, _, dq = q.shape
    _, dlat, dv = wv.shape
    tq = _pick(seq, FLASH_TQ, LANES)
    tk = _pick(seq, FLASH_TK, LANES)
    nq, nkb = seq // tq, seq // tk
    cols = nheads * tq
    kt = _key_blocks_t(k[:batch * seq], tk)
    return pl.pallas_call(
        functools.partial(_flash_mla_kernel, tq=tq, tk=tk, nheads=nheads, dlat=dlat, dv=dv),
        grid=(batch, nq),
        in_specs=[pl.BlockSpec((nheads, tq, dq), lambda b, i: (0, b * nq + i, 0)),
                  pl.BlockSpec((seq, dq), lambda b, i: (b, 0)),
                  pl.BlockSpec((nkb, dq, tk), lambda b, i: (b, 0, 0)),
                  pl.BlockSpec(wv.shape, lambda b, i: (0, 0, 0))],
        out_specs=pl.BlockSpec((tq, nheads * dv), lambda b, i: (b * nq + i, 0)),
        out_shape=jax.ShapeDtypeStruct((batch * seq, nheads * dv), BF16),
        scratch_shapes=[pltpu.VMEM((1, cols), F32), pltpu.VMEM((1, cols), F32), pltpu.VMEM((dlat, cols), F32)],
        compiler_params=_params("parallel", "parallel"),
        name="flash_mla",
    )(q, k, kt, wv)


def _diff_lambda(lam_ref, lam_init):
    lp = lam_ref[...]
    s01 = jnp.sum(lp[0:1] * lp[1:2], axis=1, keepdims=True)
    s23 = jnp.sum(lp[2:3] * lp[3:4], axis=1, keepdims=True)
    return jnp.exp(s01) - jnp.exp(s23) + lam_init


def _diff_combine(o1, o2, lam, g, lam_init):
    return _rms_rows(o1 - lam * o2, g) * (1.0 - lam_init)


def _flash_diff_kernel(q_ref, k_ref, vt_ref, lam_ref, g_ref, o_ref, m_ref, l_ref, acc_ref, *, tq, tk, nh, dd, lam_init):
    cols = nh * tq
    q = q_ref[...]
    qs = [jnp.concatenate([q[:, (2 * h + j) * dd:(2 * h + j + 1) * dd] for h in range(nh)], axis=0) for j in range(2)]
    q_start = pl.program_id(1) * tq
    tok = _query_tokens(q_start, tq, cols)
    m_ref[...] = jnp.full_like(m_ref, NEG)
    l_ref[...] = jnp.zeros_like(l_ref)
    acc_ref[...] = jnp.zeros_like(acc_ref)

    def step(kb, masked):
        off = pl.multiple_of(kb * tk, tk)
        k = k_ref[pl.ds(off, tk), :].astype(BF16)
        vt = vt_ref[kb]
        for j in range(2):
            st = _mask_keys(_dot_nt(k[:, j * dd:(j + 1) * dd], qs[j]), off, tok, masked)
            _online_softmax_step(m_ref.at[j], l_ref.at[j], acc_ref.at[j], st, lambda p: _dot(vt, p), axis=0)

    _causal_blocks(q_start, tq, tk, step)
    lam = _diff_lambda(lam_ref, lam_init)
    ot = acc_ref[0] / l_ref[0] - lam * (acc_ref[1] / l_ref[1])
    rt = ot * lax.rsqrt(jnp.mean(ot * ot, axis=0, keepdims=True) + NORM_EPS) * g_ref[...] * (1.0 - lam_init)
    for h in range(nh):
        o_ref[:, h * 2 * dd:(h + 1) * 2 * dd] = rt[:, h * tq:(h + 1) * tq].T.astype(o_ref.dtype)


def _flash_diff(q, k, v, lam_p, g_sub, batch, seq, lam_init):
    dd = k.shape[1] // 2
    nh = q.shape[1] // (2 * dd)
    tq = _pick(seq, FLASH_TQ, LANES)
    tk = _pick(seq, FLASH_TK, LANES)
    nq, nkb = seq // tq, seq // tk
    cols = nh * tq
    vt = _key_blocks_t(v[:batch * seq].astype(BF16), tk)
    g_col = g_sub.reshape(2 * dd, 1)
    return pl.pallas_call(
        functools.partial(_flash_diff_kernel, tq=tq, tk=tk, nh=nh, dd=dd, lam_init=lam_init),
        grid=(batch, nq),
        in_specs=[pl.BlockSpec((tq, q.shape[1]), lambda b, i: (b * nq + i, 0)),
                  pl.BlockSpec((seq, 2 * dd), lambda b, i: (b, 0)),
                  pl.BlockSpec((nkb, 2 * dd, tk), lambda b, i: (b, 0, 0)),
                  pl.BlockSpec(lam_p.shape, lambda b, i: (0, 0)),
                  pl.BlockSpec(g_col.shape, lambda b, i: (0, 0))],
        out_specs=pl.BlockSpec((tq, nh * 2 * dd), lambda b, i: (b * nq + i, 0)),
        out_shape=jax.ShapeDtypeStruct((batch * seq, nh * 2 * dd), BF16),
        scratch_shapes=[pltpu.VMEM((2, 1, cols), F32), pltpu.VMEM((2, 1, cols), F32), pltpu.VMEM((2, 2 * dd, cols), F32)],
        compiler_params=_params("parallel", "parallel"),
        name="flash_diff",
    )(q, k, vt, lam_p, g_col)


def _single_key_update(m_ref, l_ref, acc_ref, s_new, v_new):
    m_prev = m_ref[...]
    m_new = jnp.maximum(m_prev, s_new)
    alpha = jnp.exp(m_prev - m_new)
    p = jnp.exp(s_new - m_new)
    l = alpha * l_ref[...] + p
    acc = alpha * acc_ref[...] + p.astype(BF16).astype(F32) * v_new
    return acc / l


def _decode_attn_kernel(pt_ref, q_ref, dq_ref, kn_ref, dkn_ref, dvn_ref, lam_ref, g_ref, *rest,
                        pps, page, nh_d, dd, dlat, row, lam_init):
    del pt_ref
    mla_pages, dk_pages, dv_pages = rest[:pps], rest[pps:2 * pps], rest[2 * pps:3 * pps]
    olat_ref, od_ref = rest[3 * pps:3 * pps + 2]
    kbuf, m1, l1, a1, m2, l2, a2 = rest[3 * pps + 2:]
    c = pl.program_id(1)

    @pl.when(c == 0)
    def _():
        kbuf[...] = jnp.zeros_like(kbuf)
        for m_ref, l_ref, a_ref in ((m1, l1, a1), (m2, l2, a2)):
            m_ref[...] = jnp.full_like(m_ref, NEG)
            l_ref[...] = jnp.zeros_like(l_ref)
            a_ref[...] = jnp.zeros_like(a_ref)

    for r in range(pps):
        kbuf[0:row, r * page:(r + 1) * page] = mla_pages[r][...].astype(BF16)
    kt = kbuf[...]
    q = q_ref[...]
    _online_softmax_step(m1, l1, a1, _dot(q, kt), lambda p: _dot_nt(p, kt[0:dlat]))

    k1 = jnp.concatenate([p[pl.ds(0, page, stride=2), :] for p in dk_pages], axis=0).astype(BF16)
    k2 = jnp.concatenate([p[pl.ds(1, page, stride=2), :] for p in dk_pages], axis=0).astype(BF16)
    dv = jnp.concatenate([p[...] for p in dv_pages], axis=0).astype(BF16)
    dq = dq_ref[...]
    first = lax.broadcasted_iota(jnp.int32, (2 * nh_d, 1), 0) < nh_d
    s2 = jnp.where(first, _dot_nt(dq, k1), _dot_nt(dq, k2))
    _online_softmax_step(m2, l2, a2, s2, lambda p: _dot(p, dv))

    @pl.when(c == pl.num_programs(1) - 1)
    def _():
        kn = kn_ref[...].astype(F32)
        sn = jnp.sum(q.astype(F32) * kn, axis=1, keepdims=True)
        olat_ref[...] = _single_key_update(m1, l1, a1, sn, kn[:, :dlat])
        dkn = dkn_ref[...].astype(BF16).astype(F32)
        dvn = dvn_ref[...].astype(BF16).astype(F32)
        dqf = dq.astype(F32)
        sn2 = jnp.where(first, jnp.sum(dqf * dkn[:, :dd], axis=1, keepdims=True),
                        jnp.sum(dqf * dkn[:, dd:], axis=1, keepdims=True))
        on = _single_key_update(m2, l2, a2, sn2, dvn)
        lam = _diff_lambda(lam_ref, lam_init)
        od_ref[...] = _diff_combine(on[:nh_d], on[nh_d:], lam, g_ref[...], lam_init).astype(od_ref.dtype)


def _decode_attn(page_table, q, dq, k_new, dk_new, dv_new, lam_p, g_sub, cache_mla, cache_dk, cache_dv, lam_init):
    nb, n_pages = page_table.shape
    n_pool, page, row = cache_mla.shape
    nheads, dq_w = q.shape[1:]
    dd = cache_dv.shape[-1] // 2
    nh_d = dq.shape[1] // 2
    dlat = dq_w - LANES
    pps = _pick(n_pages, DECODE_PAGES, 1)
    cache_mla = jnp.transpose(cache_mla, (0, 2, 1))
    cache_dk = cache_dk.reshape(n_pool, 2 * page, dd)

    def page_spec(rows, width, r):
        return pl.BlockSpec((None, rows, width), lambda b, c, pt: (pt[b * n_pages + c * pps + r], 0, 0))

    per_seq = lambda a: pl.BlockSpec((None,) + a.shape[1:], lambda b, c, pt: (b, 0, 0))
    full = lambda a: pl.BlockSpec(a.shape, lambda b, c, pt: (0, 0))
    grid_spec = pltpu.PrefetchScalarGridSpec(
        num_scalar_prefetch=1,
        grid=(nb, n_pages // pps),
        in_specs=([per_seq(q), per_seq(dq), per_seq(k_new), per_seq(dk_new), per_seq(dv_new), full(lam_p), full(g_sub)]
                  + [page_spec(row, page, r) for r in range(pps)]
                  + [page_spec(2 * page, dd, r) for r in range(pps)]
                  + [page_spec(page, 2 * dd, r) for r in range(pps)]),
        out_specs=[pl.BlockSpec((None, nheads, dlat), lambda b, c, pt: (b, 0, 0)),
                   pl.BlockSpec((None, nh_d, 2 * dd), lambda b, c, pt: (b, 0, 0))],
        scratch_shapes=[pltpu.VMEM((dq_w, pps * page), BF16),
                        pltpu.VMEM((nheads, 1), F32), pltpu.VMEM((nheads, 1), F32), pltpu.VMEM((nheads, dlat), F32),
                        pltpu.VMEM((2 * nh_d, 1), F32), pltpu.VMEM((2 * nh_d, 1), F32), pltpu.VMEM((2 * nh_d, 2 * dd), F32)],
    )
    return pl.pallas_call(
        functools.partial(_decode_attn_kernel, pps=pps, page=page, nh_d=nh_d, dd=dd, dlat=dlat, row=row, lam_init=lam_init),
        grid_spec=grid_spec,
        out_shape=[jax.ShapeDtypeStruct((nb, nheads, dlat), F32), jax.ShapeDtypeStruct((nb, nh_d, 2 * dd), BF16)],
        compiler_params=_params("parallel", "arbitrary"),
        name="decode_attn",
    )(page_table.reshape(-1), q, dq, k_new, dk_new, dv_new, lam_p, g_sub,
      *([cache_mla] * pps), *([cache_dk] * pps), *([cache_dv] * pps))


def _head_proj_kernel(x_ref, w_ref, o_ref):
    o_ref[...] = _dot(x_ref[...].astype(BF16), w_ref[...]).astype(o_ref.dtype)


def _head_proj(x, w):
    nheads, n, c = x.shape
    v = w.shape[2]
    return pl.pallas_call(
        _head_proj_kernel,
        grid=(nheads,),
        in_specs=[pl.BlockSpec((None, n, c), lambda h: (h, 0, 0)), pl.BlockSpec((None, c, v), lambda h: (h, 0, 0))],
        out_specs=pl.BlockSpec((n, v), lambda h: (0, h)),
        out_shape=jax.ShapeDtypeStruct((n, nheads * v), BF16),
        compiler_params=_params("parallel"),
        name="head_proj",
    )(x, w)


def _softplus(x):
    return jnp.maximum(x, 0.0) + jnp.log1p(jnp.exp(-jnp.abs(x)))


def _dtprep_kernel(raw_ref, bias_ref, alog_ref, dt_ref, da_ref, cum_ref):
    dt = _softplus(raw_ref[...] + bias_ref[...])
    da = dt * -jnp.exp(alog_ref[...])
    dt_ref[...] = dt
    da_ref[...] = da
    rows = da.shape[0]
    row = lax.broadcasted_iota(jnp.int32, da.shape, 0)
    cum = da
    shift = 1
    while shift < rows:
        cum = cum + jnp.where(row >= shift, pltpu.roll(cum, shift, 0), 0.0)
        shift *= 2
    cum_ref[...] = cum


def _dtprep(raw, bias, a_log, chunk):
    m, nh = raw.shape
    spec = pl.BlockSpec((chunk, nh), lambda i: (i, 0))
    vec = pl.BlockSpec((1, nh), lambda i: (0, 0))
    shape = jax.ShapeDtypeStruct((m, nh), F32)
    return pl.pallas_call(
        _dtprep_kernel,
        grid=(m // chunk,),
        in_specs=[spec, vec, vec],
        out_specs=[spec, spec, spec],
        out_shape=[shape, shape, shape],
        compiler_params=_params("parallel"),
        name="ssd_dtprep",
    )(raw, bias.reshape(1, nh), a_log.reshape(1, nh))


def _conv_kernel(x_ref, halo_ref, w_ref, b_ref, o_ref, buf, *, tt, width, halo):
    j = pl.program_id(1)
    buf[0:halo, :] = jnp.where(j == 0, 0.0, halo_ref[...])
    buf[halo:halo + tt, :] = x_ref[...]
    acc = b_ref[...] + buf[pl.ds(halo - (width - 1), tt), :] * w_ref[0:1, :]
    for k in range(1, width):
        acc = acc + buf[pl.ds(halo - (width - 1) + k, tt), :] * w_ref[k:k + 1, :]
    o_ref[...] = _silu(acc)


def _conv_prompt(x, w, b, batch, seq):
    cdim = x.shape[1]
    width = w.shape[0]
    halo = 8
    tt = _pick(seq, 256, 8)
    tc = _pick(cdim, 2048, LANES)
    nt = seq // tt
    return pl.pallas_call(
        functools.partial(_conv_kernel, tt=tt, width=width, halo=halo),
        grid=(batch, nt, cdim // tc),
        in_specs=[pl.BlockSpec((tt, tc), lambda b_, j, c: (b_ * nt + j, c)),
                  pl.BlockSpec((halo, tc), lambda b_, j, c: (jnp.maximum((b_ * nt + j) * (tt // halo) - 1, 0), c)),
                  pl.BlockSpec((width, tc), lambda b_, j, c: (0, c)),
                  pl.BlockSpec((1, tc), lambda b_, j, c: (0, c))],
        out_specs=pl.BlockSpec((tt, tc), lambda b_, j, c: (b_ * nt + j, c)),
        out_shape=jax.ShapeDtypeStruct((batch * seq, cdim), F32),
        scratch_shapes=[pltpu.VMEM((halo + tt, tc), F32)],
        compiler_params=_params("parallel", "parallel", "parallel"),
        name="ssd_conv",
    )(x, x, w, b.reshape(1, cdim))


def _conv_decode_kernel(s_ref, x_ref, w_ref, b_ref, o_ref, *, width):
    acc = b_ref[...] + s_ref[0] * w_ref[0:1, :]
    for k in range(1, width - 1):
        acc = acc + s_ref[k] * w_ref[k:k + 1, :]
    o_ref[...] = _silu(acc + x_ref[...] * w_ref[width - 1:width, :])


def _conv_decode(state_t, x, w, b):
    n, cdim = x.shape
    width = w.shape[0]
    tc = _pick(cdim, 2048, LANES)
    return pl.pallas_call(
        functools.partial(_conv_decode_kernel, width=width),
        grid=(cdim // tc,),
        in_specs=[pl.BlockSpec((width - 1, n, tc), lambda c: (0, 0, c)),
                  pl.BlockSpec((n, tc), lambda c: (0, c)),
                  pl.BlockSpec((width, tc), lambda c: (0, c)),
                  pl.BlockSpec((1, tc), lambda c: (0, c))],
        out_specs=pl.BlockSpec((n, tc), lambda c: (0, c)),
        out_shape=jax.ShapeDtypeStruct((n, cdim), F32),
        compiler_params=_params("parallel"),
        name="ssd_conv_decode",
    )(state_t, x, w, b.reshape(1, cdim))


def _gate_norm(y, z, g):
    return _rms_rows(y * _silu(z), g)


def _ssd_chunk_kernel(x_ref, b_ref, c_ref, z_ref, cumt_ref, dtt_ref, dskip_ref, g_ref,
                      y_ref, hout_ref, h_ref, y_scr, *, chunk, hpg, hd):
    ci = pl.program_id(2)

    @pl.when(ci == 0)
    def _():
        h_ref[...] = jnp.zeros_like(h_ref)

    gw = hpg * hd
    x = x_ref[...]
    bmat = b_ref[...].astype(BF16)
    cmat = c_ref[...].astype(BF16)
    cb = _dot_nt(cmat, bmat)
    h_all = h_ref[...]
    y_inter = _dot_nt(cmat, h_all.astype(BF16))
    cumt = cumt_ref[...]
    dtt = dtt_ref[...]

    srow = lax.broadcasted_iota(jnp.int32, (3 * hpg, hpg * LANES), 0) % hpg
    scol = lax.broadcasted_iota(jnp.int32, (3 * hpg, hpg * LANES), 1) // LANES
    cumb = _dot_tn(_split3(cumt), jnp.where(srow == scol, 1.0, 0.0).astype(BF16))

    causal = (lax.broadcasted_iota(jnp.int32, (chunk, chunk), 0) >= lax.broadcasted_iota(jnp.int32, (chunk, chunk), 1))
    even_head = (lax.broadcasted_iota(jnp.int32, (1, gw), 1) // hd) % 2 == 0
    x_even = jnp.where(even_head, x, 0.0).astype(BF16)
    x_odd = jnp.where(even_head, 0.0, x).astype(BF16)
    first_half = lax.broadcasted_iota(jnp.int32, (1, LANES), 1) < hd
    for p in range(hpg // 2):
        blk = slice(p * LANES, (p + 1) * LANES)
        y_pair = None
        for r, xm in ((2 * p, x_even), (2 * p + 1, x_odd)):
            cum_i = cumb[:, r * LANES:(r + 1) * LANES]
            seg = jnp.concatenate([cum_i - cumt[r:r + 1, k * LANES:(k + 1) * LANES] for k in range(chunk // LANES)], axis=1)
            wgt = cb * jnp.exp(jnp.where(causal, seg, -jnp.inf)) * dtt[r:r + 1, :]
            part = _dot(wgt.astype(BF16), xm[:, blk])
            y_pair = part if y_pair is None else y_pair + part
        cum_pair = jnp.where(first_half, cumb[:, 2 * p * LANES:(2 * p + 1) * LANES], cumb[:, (2 * p + 1) * LANES:(2 * p + 2) * LANES])
        y_scr[:, blk] = y_pair + y_inter[:, blk] * jnp.exp(cum_pair)

    xt = x.T
    scaled, decay = [], []
    for r in range(hpg):
        last = cumt[r:r + 1, chunk - 1:chunk]
        to_end = jnp.exp(last - cumt[r:r + 1, :]) * dtt[r:r + 1, :]
        scaled.append((xt[r * hd:(r + 1) * hd, :] * to_end).astype(BF16))
        decay.append(jnp.broadcast_to(jnp.exp(last), (hd, h_all.shape[1])))
    h_ref[...] = h_all * jnp.concatenate(decay, axis=0) + _dot(jnp.concatenate(scaled, axis=0), bmat)

    y = y_scr[...] + dskip_ref[...] * x
    y_ref[...] = _gate_norm(y, z_ref[...], g_ref[...]).astype(y_ref.dtype)

    @pl.when(ci == pl.num_programs(2) - 1)
    def _():
        hout_ref[...] = h_ref[...]


def _ssd_prompt(xbc, z, cum_t, dt_t, dskip, g_norm, batch, seq, chunk, ngroups, nstate):
    d_inner = z.shape[1]
    gw = d_inner // ngroups
    hpg = cum_t.shape[0] // ngroups
    hd = gw // hpg
    assert 2 * hd == LANES and hpg % 2 == 0 and chunk % LANES == 0
    nc = seq // chunk
    xoff = d_inner // nstate
    rows = lambda b, g, c: b * nc + c
    return pl.pallas_call(
        functools.partial(_ssd_chunk_kernel, chunk=chunk, hpg=hpg, hd=hd),
        grid=(batch, ngroups, nc),
        in_specs=[pl.BlockSpec((chunk, gw), lambda b, g, c: (rows(b, g, c), g)),
                  pl.BlockSpec((chunk, nstate), lambda b, g, c: (rows(b, g, c), xoff + g)),
                  pl.BlockSpec((chunk, nstate), lambda b, g, c: (rows(b, g, c), xoff + ngroups + g)),
                  pl.BlockSpec((chunk, gw), lambda b, g, c: (rows(b, g, c), g)),
                  pl.BlockSpec((hpg, chunk), lambda b, g, c: (g, rows(b, g, c))),
                  pl.BlockSpec((hpg, chunk), lambda b, g, c: (g, rows(b, g, c))),
                  pl.BlockSpec((1, gw), lambda b, g, c: (0, g)),
                  pl.BlockSpec((1, gw), lambda b, g, c: (0, g))],
        out_specs=[pl.BlockSpec((chunk, gw), lambda b, g, c: (rows(b, g, c), g)),
                   pl.BlockSpec((None, gw, nstate), lambda b, g, c: (b * ngroups + g, 0, 0))],
        out_shape=[jax.ShapeDtypeStruct((batch * seq, d_inner), BF16),
                   jax.ShapeDtypeStruct((batch * ngroups, gw, nstate), F32)],
        scratch_shapes=[pltpu.VMEM((gw, nstate), F32), pltpu.VMEM((chunk, gw), F32)],
        compiler_params=_params("parallel", "parallel", "arbitrary"),
        name="ssd_chunk_scan",
    )(xbc, xbc, xbc, z, cum_t, dt_t, dskip, g_norm)


def _split3(a):
    hi = a.astype(BF16)
    r1 = a - hi.astype(F32)
    mid = r1.astype(BF16)
    lo = (r1 - mid.astype(F32)).astype(BF16)
    return jnp.concatenate([hi, mid, lo], axis=0)


def _ssd_step_kernel(h_ref, x_ref, b_ref, c_ref, z_ref, dt_ref, da_ref, dskip_ref, g_ref, hout_ref, y_ref, y_scr,
                     *, sb, nstate):
    gw = x_ref.shape[1]
    x = x_ref[...]
    xdt = x * dt_ref[...]
    decay = jnp.exp(da_ref[...])
    prow = lax.broadcasted_iota(jnp.int32, (16, gw), 0)
    srow = lax.broadcasted_iota(jnp.int32, (48, 2 * nstate), 0) % 16
    scol = lax.broadcasted_iota(jnp.int32, (48, 2 * nstate), 1)
    sel = jnp.where(((srow == 0) & (scol < nstate)) | ((srow == 1) & (scol >= nstate)), 1.0, 0.0).astype(BF16)
    for s in range(sb):
        pair = jnp.where(prow == 0, xdt[s:s + 1, :], jnp.where(prow == 1, decay[s:s + 1, :], 0.0))
        bc = _dot_tn(_split3(pair), sel)
        hn = h_ref[s].reshape(gw, nstate) * bc[:, nstate:] + bc[:, :nstate] * b_ref[s:s + 1, :]
        hout_ref[s] = hn.reshape(hout_ref.shape[1:])
        crow = jnp.broadcast_to(c_ref[s:s + 1, :], (16, nstate)).astype(BF16)
        y_scr[s:s + 1, :] = _dot_nt(crow, hn.astype(BF16))[0:1]
    y = y_scr[...] + dskip_ref[...] * x
    y_ref[...] = _gate_norm(y, z_ref[...], g_ref[...]).astype(y_ref.dtype)


def _ssd_step(h, act, z, z_row0, dt_rep, da_rep, dskip, g_norm, ngroups):
    n, nheads, hd, nstate = h.shape
    d_inner = nheads * hd
    hpg = nheads // ngroups
    gw = hpg * hd
    sb = _pick(n, 8, 8)
    assert z_row0 % sb == 0
    xoff = d_inner // nstate
    hspec = pl.BlockSpec((sb, hpg, hd, nstate), lambda i, g: (i, g, 0, 0))
    chan = pl.BlockSpec((sb, gw), lambda i, g: (i, g))
    vec = pl.BlockSpec((1, gw), lambda i, g: (0, g))
    return pl.pallas_call(
        functools.partial(_ssd_step_kernel, sb=sb, nstate=nstate),
        grid=(n // sb, ngroups),
        in_specs=[hspec, chan,
                  pl.BlockSpec((sb, nstate), lambda i, g: (i, xoff + g)),
                  pl.BlockSpec((sb, nstate), lambda i, g: (i, xoff + ngroups + g)),
                  pl.BlockSpec((sb, gw), lambda i, g: (z_row0 // sb + i, g)),
                  chan, chan, vec, vec],
        out_specs=[hspec, chan],
        out_shape=[jax.ShapeDtypeStruct(h.shape, F32), jax.ShapeDtypeStruct((n, d_inner), F32)],
        scratch_shapes=[pltpu.VMEM((sb, gw), F32)],
        compiler_params=_params("parallel", "parallel"),
        name="ssd_step",
    )(h, act, act, act, z, dt_rep, da_rep, dskip, g_norm)


def _rot_cols(w):
    half = w.shape[-1] // 2
    return jnp.concatenate([-w[..., half:], w[..., :half]], axis=-1)


def _attn_layer(h, layer, mp, seq, batch, page_table, caches, cos, sin, w_in, g_q, g_kv, w_qb, w_kvb, lam_p, g_sub, w_out):
    cache_mla, cache_dk, cache_dv = caches
    q_lora = g_q.shape[0]
    dlat = g_kv.shape[0]
    rope = cache_mla.shape[-1] - dlat
    dd = cache_dv.shape[-1] // 2
    nheads = w_qb.shape[1]
    nope = w_qb.shape[2] - rope
    dv = w_kvb.shape[2] - nope
    assert 2 * rope == LANES, "the rope rotation is done inside one 128-lane chunk"
    lam_init = 0.8 - 0.6 * math.exp(-0.3 * layer)
    nd = h.shape[0] - mp

    o_ckv, o_kpe, o_dq = q_lora, q_lora + dlat, q_lora + dlat + rope
    o_dk = w_in.shape[1] - 4 * dd
    w_cq = w_in[:, :q_lora].astype(BF16)
    w_kpe = w_in[:, o_kpe:o_dq]
    w_ckr = jnp.concatenate([w_in[:, o_ckv:o_kpe], w_kpe, _rot_cols(w_kpe)], axis=1).astype(BF16)
    w_dq = w_in[:, o_dq:o_dk].astype(BF16)
    w_dk = w_in[:, o_dk:o_dk + 2 * dd].astype(BF16)
    w_dv = w_in[:, o_dk + 2 * dd:].astype(BF16)
    w_pe = w_qb[..., nope:]
    wq = jnp.concatenate([w_qb[..., :nope].reshape(q_lora, nheads * nope),
                          jnp.concatenate([w_pe, _rot_cols(w_pe)], axis=-1).reshape(q_lora, nheads * LANES)],
                         axis=1).astype(BF16)
    wk = jnp.transpose(w_kvb[..., :nope], (1, 2, 0)).astype(BF16)
    wv = jnp.transpose(w_kvb[..., nope:], (1, 0, 2)).astype(BF16)

    cq = _matmul(h, w_cq)
    ckr = _matmul(h, w_ckr)
    dq = _matmul(h, w_dq, out_dtype=BF16, scale=dd ** -0.5)
    dk = _matmul(h, w_dk)
    dv_rows = _matmul(h, w_dv)

    q = _qprep(cq, g_q.reshape(1, -1), wq, wk, cos, sin, (nope + rope) ** -0.5)
    mla_row, k_pad = _kvprep(ckr, g_kv.reshape(1, -1), cos, sin, rope)

    g_sub2 = g_sub.reshape(1, -1)
    o_mla_p = _flash_mla(q, k_pad, wv, batch, seq)
    o_d_p = _flash_diff(dq, dk, dv_rows, lam_p, g_sub2, batch, seq, lam_init)

    q_s = jnp.transpose(q[:, mp:], (1, 0, 2))
    nh_d = dq.shape[1] // (2 * dd)
    dq_s = jnp.transpose(dq[mp:].reshape(nd, nh_d, 2, dd), (0, 2, 1, 3)).reshape(nd, 2 * nh_d, dd)
    o_lat_s, o_d_s = _decode_attn(page_table, q_s, dq_s, k_pad[mp:, None], dk[mp:, None], dv_rows[mp:, None],
                                  lam_p, g_sub2, cache_mla, cache_dk, cache_dv, lam_init)
    o_mla_s = _head_proj(jnp.transpose(o_lat_s, (1, 0, 2)), wv)

    o = jnp.concatenate([jnp.concatenate([o_mla_p, o_d_p], axis=1),
                         jnp.concatenate([o_mla_s, o_d_s.reshape(nd, -1)], axis=1)], axis=0)
    out = _matmul(o, w_out, layer=layer // 2)
    new = (mla_row[:mp].reshape(batch, seq, -1), dk[:mp].reshape(batch, seq, 2, dd), dv_rows[:mp].reshape(batch, seq, -1),
           mla_row[mp:].reshape(nd, 1, -1), dk[mp:].reshape(nd, 1, 2, dd), dv_rows[mp:].reshape(nd, 1, -1))
    return out, new


def _ssd_layer(h, j, mp, seq, batch, conv_state, ssm_state, w_in, conv_w, conv_b, dt_bias, a_log, d_skip, g_norm, w_out):
    nd, nheads, hd, nstate = ssm_state.shape
    d_inner = nheads * hd
    cdim = conv_w.shape[1]
    ngroups = (cdim - d_inner) // (2 * nstate)
    hpg = nheads // ngroups
    chunk = _pick(seq, 256, 8)
    width = conv_w.shape[0]

    z = _matmul(h, w_in, layer=j, cols=(0, d_inner))
    xbc = _matmul(h, w_in, layer=j, cols=(d_inner, cdim))
    dt_raw = _matmul(h, w_in, layer=j, cols=(d_inner + cdim, nheads))

    dskip = jnp.repeat(d_skip, hd).reshape(1, d_inner)
    g2 = g_norm.reshape(1, d_inner)

    dt_p, _, cum_p = _dtprep(dt_raw[:mp], dt_bias, a_log, chunk)
    xbc_act = _conv_prompt(xbc, conv_w, conv_b, batch, seq)
    y_p, h_p = _ssd_prompt(xbc_act, z, cum_p.T, dt_p.T, dskip, g2, batch, seq, chunk, ngroups, nstate)
    conv_p = jnp.stack([xbc[b * seq + seq - (width - 1):(b + 1) * seq] for b in range(batch)])
    ssm_p = h_p.reshape(batch, nheads, hd, nstate)

    xbc_s = xbc[mp:]
    dt_s, da_s, _ = _dtprep(dt_raw[mp:], dt_bias, a_log, nd)
    act_s = _conv_decode(jnp.transpose(conv_state, (1, 0, 2)), xbc_s, conv_w, conv_b)
    ssm_s, y_s = _ssd_step(ssm_state, act_s, z, mp, jnp.repeat(dt_s, hd, axis=1), jnp.repeat(da_s, hd, axis=1),
                           dskip, g2, ngroups)
    conv_s = jnp.concatenate([conv_state[:, 1:], xbc_s[:, None]], axis=1)

    out = _matmul(jnp.concatenate([y_p, y_s.astype(BF16)], axis=0), w_out, layer=j)
    return out, (conv_p, ssm_p, conv_s, ssm_s)


def kernel(x_prompt, x_sample, cache_mla_l0, cache_dk_l0, cache_dv_l0, state_conv_l1, state_ssm_l1, cache_mla_l2, cache_dk_l2, cache_dv_l2, state_conv_l3, state_ssm_l3, page_table, norm_mix_pre, norm_mix_post, norm_ffn_pre, norm_ffn_post, ffn_w_gate, ffn_w_up, ffn_w_down, attn_w_in, mla_g_q, mla_g_kv, mla_w_qb, mla_w_kvb, diff_lambda, diff_g_sub, attn_w_out, ssm_w_in, ssm_conv_w, ssm_conv_b, ssm_dt_bias, ssm_a_log, ssm_d, ssm_g_norm, ssm_w_out):
    batch, seq, d_model = x_prompt.shape
    nd, dec_seq, _ = x_sample.shape
    assert dec_seq == 1, "decode rows are one new token per sequence"
    mp = batch * seq
    depth = norm_mix_pre.shape[0]
    attn_caches = {0: (cache_mla_l0, cache_dk_l0, cache_dv_l0), 2: (cache_mla_l2, cache_dk_l2, cache_dv_l2)}
    ssm_states = {1: (state_conv_l1, state_ssm_l1), 3: (state_conv_l3, state_ssm_l3)}

    rope = cache_mla_l0.shape[-1] - mla_g_kv.shape[1]
    half = rope // 2
    past_len = page_table.shape[1] * cache_mla_l0.shape[1]
    pos = jnp.concatenate([jnp.tile(jnp.arange(seq), batch), jnp.full((nd,), past_len)])
    inv = jnp.exp(-math.log(ROPE_BASE) * jnp.arange(half, dtype=F32) / half)
    ang = pos.astype(F32)[:, None] * inv[None, :]
    pad = jnp.zeros((mp + nd, LANES - rope), F32)
    cos = jnp.concatenate([jnp.cos(ang), jnp.cos(ang), pad], axis=1)
    sin = jnp.concatenate([jnp.sin(ang), jnp.sin(ang), pad], axis=1)

    d_ff = ffn_w_gate.shape[2]
    ff_pad = -d_ff % FFN_ALIGN
    wg = _cast_pad_cols(ffn_w_gate, ff_pad)
    wu = _cast_pad_cols(ffn_w_up, ff_pad)
    wd = _cast_pad_rows(ffn_w_down, ff_pad)
    w_attn_out = attn_w_out.astype(BF16)
    w_ssm_in = ssm_w_in.astype(BF16)
    w_ssm_out = ssm_w_out.astype(BF16)

    x = jnp.concatenate([x_prompt.reshape(mp, d_model), x_sample.reshape(nd, d_model)], axis=0)
    h = _rms(x, norm_mix_pre[0])
    new = {}
    for layer in range(depth):
        j = layer // 2
        if layer % 2 == 0:
            o, new[layer] = _attn_layer(h, layer, mp, seq, batch, page_table, attn_caches[layer], cos, sin,
                                        attn_w_in[j], mla_g_q[j], mla_g_kv[j], mla_w_qb[j], mla_w_kvb[j],
                                        diff_lambda[j], diff_g_sub[j], w_attn_out)
        else:
            o, new[layer] = _ssd_layer(h, j, mp, seq, batch, *ssm_states[layer], w_ssm_in, ssm_conv_w[j], ssm_conv_b[j],
                                       ssm_dt_bias[j], ssm_a_log[j], ssm_d[j], ssm_g_norm[j], w_ssm_out)
        x, h = _add_rms(x, o, norm_mix_post[layer], norm_ffn_pre[layer])
        o = _matmul(_swiglu_up(h, wg, wu, layer), wd, layer=layer, bk=2816)
        x, h = _add_rms(x, o, norm_ffn_post[layer], norm_mix_pre[layer + 1] if layer + 1 < depth else None)
    outs = [x[:mp].reshape(batch, seq, d_model), x[mp:].reshape(nd, 1, d_model)]
    for layer in range(depth):
        outs.extend(new[layer])
    return tuple(outs)
```

```python
import functools
import math

import jax
import jax.numpy as jnp
from jax import lax
from jax.experimental import pallas as pl
from jax.experimental.pallas import tpu as pltpu

F32 = jnp.float32
BF16 = jnp.bfloat16
NORM_EPS = 1e-6
ROPE_BASE = 10000.0
NEG = -1e30
LANES = 128
VMEM_LIMIT = 56 * 1024 * 1024
FFN_ALIGN = 1024
FLASH_TQ = 128
FLASH_TK = 512
DECODE_PAGES = 32


def _pick(n, target, align):
    if n <= target:
        return n
    d = target - target % align
    while d >= align:
        if n % d == 0:
            return d
        d -= align
    return n


def _params(*sem):
    return pltpu.CompilerParams(dimension_semantics=sem, vmem_limit_bytes=VMEM_LIMIT)


def _rms_rows(x, g):
    return x * lax.rsqrt(jnp.mean(x * x, axis=-1, keepdims=True) + NORM_EPS) * g


def _silu(x):
    return x * jax.nn.sigmoid(x)


def _dot(a, b):
    return jnp.dot(a, b, preferred_element_type=F32)


def _dot_nt(a, b):
    return lax.dot_general(a, b, (((1,), (1,)), ((), ())), preferred_element_type=F32)


def _dot_tn(a, b):
    return lax.dot_general(a, b, (((0,), (0,)), ((), ())), preferred_element_type=F32)


def _pair_specs(tm, d, nb_top):
    top = pl.BlockSpec((tm, d), lambda i: (jnp.minimum(i, nb_top - 1), 0))
    bot = pl.BlockSpec((tm, d), lambda i: (jnp.maximum(i - nb_top, 0), 0))
    return [top, bot]


def _pair_tile(top, bot, target):
    return _pick(math.gcd(top.shape[0], bot.shape[0]), target, 8)


def _rms_kernel(*refs, nb_top):
    *x_refs, g_ref, o_ref = refs
    x = x_refs[0][...] if nb_top is None else jnp.where(pl.program_id(0) < nb_top, x_refs[0][...], x_refs[1][...])
    o_ref[...] = _rms_rows(x, g_ref[...]).astype(o_ref.dtype)


def _rms(x, g, out_dtype=BF16):
    pair = isinstance(x, tuple)
    m = sum(a.shape[0] for a in x) if pair else x.shape[0]
    d = g.shape[0]
    tm = _pair_tile(*x, 512) if pair else _pick(m, 512, 16)
    nb_top = x[0].shape[0] // tm if pair else None
    row = pl.BlockSpec((tm, d), lambda i: (i, 0))
    return pl.pallas_call(
        functools.partial(_rms_kernel, nb_top=nb_top),
        grid=(m // tm,),
        in_specs=(_pair_specs(tm, d, nb_top) if pair else [row]) + [pl.BlockSpec((1, d), lambda i: (0, 0))],
        out_specs=row,
        out_shape=jax.ShapeDtypeStruct((m, d), out_dtype),
        compiler_params=_params("parallel"),
        name="rms",
    )(*(x if pair else (x,)), g.reshape(1, d))


def _add_rms_kernel(*refs, nb_in, nb_out, with_next):
    refs = list(refs)
    step = pl.program_id(0)
    if nb_in is None:
        x = refs.pop(0)[...]
    else:
        top, bot = refs.pop(0), refs.pop(0)
        x = jnp.where(step < nb_in, top[...], bot[...])
    o_ref, gpost_ref = refs.pop(0), refs.pop(0)
    gnext_ref = refs.pop(0) if with_next else None
    xn = x + _rms_rows(o_ref[...], gpost_ref[...])
    if nb_out is None:
        refs.pop(0)[...] = xn
    else:
        top, bot = refs.pop(0), refs.pop(0)

        @pl.when(step < nb_out)
        def _():
            top[...] = xn

        @pl.when(step >= nb_out)
        def _():
            bot[...] = xn
    if with_next:
        refs.pop(0)[...] = _rms_rows(xn, gnext_ref[...]).astype(BF16)


def _add_rms(x, o, g_post, g_next=None, split_rows=None):
    pair = isinstance(x, tuple)
    m, d = o.shape
    if pair:
        tm = _pair_tile(*x, 256)
    elif split_rows is not None:
        tm = _pick(math.gcd(split_rows, m - split_rows), 256, 8)
    else:
        tm = _pick(m, 256, 16)
    nb_in = x[0].shape[0] // tm if pair else None
    nb_out = split_rows // tm if split_rows is not None else None
    row = pl.BlockSpec((tm, d), lambda i: (i, 0))
    vec = pl.BlockSpec((1, d), lambda i: (0, 0))
    with_next = g_next is not None
    in_specs = (_pair_specs(tm, d, nb_in) if pair else [row]) + [row, vec] + ([vec] if with_next else [])
    args = list(x if pair else (x,)) + [o, g_post.reshape(1, d)] + ([g_next.reshape(1, d)] if with_next else [])
    if nb_out is None:
        out_specs, out_shape = [row], [jax.ShapeDtypeStruct((m, d), F32)]
    else:
        out_specs = _pair_specs(tm, d, nb_out)
        out_shape = [jax.ShapeDtypeStruct((split_rows, d), F32), jax.ShapeDtypeStruct((m - split_rows, d), F32)]
    if with_next:
        out_specs, out_shape = out_specs + [row], out_shape + [jax.ShapeDtypeStruct((m, d), BF16)]
    out = pl.pallas_call(
        functools.partial(_add_rms_kernel, nb_in=nb_in, nb_out=nb_out, with_next=with_next),
        grid=(m // tm,),
        in_specs=in_specs,
        out_specs=out_specs,
        out_shape=out_shape,
        compiler_params=_params("arbitrary" if nb_out is not None else "parallel"),
        name="add_rms",
    )(*args)
    xs = out[0] if nb_out is None else (out[0], out[1])
    return xs, (out[-1] if with_next else None)


def _mm_kernel(x_ref, w_ref, o_ref, *scratch, nk, scale):
    def finish(acc):
        return (acc * scale if scale != 1.0 else acc).astype(o_ref.dtype)

    if nk == 1:
        o_ref[...] = finish(_dot(x_ref[...], w_ref[...]))
        return
    acc_ref = scratch[0] if scratch else o_ref
    k = pl.program_id(2)

    @pl.when(k == 0)
    def _():
        acc_ref[...] = jnp.zeros_like(acc_ref)

    acc_ref[...] += _dot(x_ref[...], w_ref[...])
    if scratch or scale != 1.0:
        @pl.when(k == nk - 1)
        def _():
            o_ref[...] = finish(acc_ref[...])


def _matmul(x, w, out_dtype=F32, scale=1.0, layer=None, cols=None, bm=1040, bn=1024, bk=4096):
    m, kdim = x.shape
    n0, n = cols if cols is not None else (0, w.shape[-1])
    bm = _pick(m, bm, 16)
    bn = _pick(math.gcd(n, n0) if n0 else n, bn, LANES)
    bk = _pick(kdim, bk, LANES)
    nk = kdim // bk
    j0 = n0 // bn
    use_scratch = nk > 1 and out_dtype != F32
    if w.ndim == 3:
        wspec = pl.BlockSpec((None, bk, bn), lambda i, j, k: (layer, k, j0 + j))
    else:
        wspec = pl.BlockSpec((bk, bn), lambda i, j, k: (k, j0 + j))
    return pl.pallas_call(
        functools.partial(_mm_kernel, nk=nk, scale=scale),
        grid=(m // bm, n // bn, nk),
        in_specs=[pl.BlockSpec((bm, bk), lambda i, j, k: (i, k)), wspec],
        out_specs=pl.BlockSpec((bm, bn), lambda i, j, k: (i, j)),
        out_shape=jax.ShapeDtypeStruct((m, n), out_dtype),
        scratch_shapes=[pltpu.VMEM((bm, bn), F32)] if use_scratch else [],
        compiler_params=_params("parallel", "parallel", "arbitrary"),
        name="matmul",
    )(x, w)


def _cast_pad_cols_kernel(w_ref, o_ref):
    n = w_ref.shape[1]
    o_ref[:, 0:n] = w_ref[...].astype(o_ref.dtype)
    if o_ref.shape[1] > n:
        o_ref[:, n:] = jnp.zeros((o_ref.shape[0], o_ref.shape[1] - n), o_ref.dtype)


def _cast_pad_cols(w, pad):
    nl, kdim, n = w.shape
    tr = _pick(kdim, 256, 16)
    return pl.pallas_call(
        _cast_pad_cols_kernel,
        grid=(nl, kdim // tr),
        in_specs=[pl.BlockSpec((None, tr, n), lambda l, i: (l, i, 0))],
        out_specs=pl.BlockSpec((None, tr, n + pad), lambda l, i: (l, i, 0)),
        out_shape=jax.ShapeDtypeStruct((nl, kdim, n + pad), BF16),
        compiler_params=_params("parallel", "parallel"),
        name="cast_pad_cols",
    )(w)


def _cast_pad_rows_kernel(w_ref, o_ref, *, n_valid):
    keep = pl.program_id(1) < n_valid
    o_ref[...] = jnp.where(keep, w_ref[...], 0.0).astype(o_ref.dtype)


def _cast_pad_rows(w, pad):
    nl, kdim, n = w.shape
    tr = math.gcd(kdim, pad) if pad else _pick(kdim, 256, 16)
    n_valid = kdim // tr
    return pl.pallas_call(
        functools.partial(_cast_pad_rows_kernel, n_valid=n_valid),
        grid=(nl, (kdim + pad) // tr),
        in_specs=[pl.BlockSpec((None, tr, n), lambda l, i: (l, jnp.minimum(i, n_valid - 1), 0))],
        out_specs=pl.BlockSpec((None, tr, n), lambda l, i: (l, i, 0)),
        out_shape=jax.ShapeDtypeStruct((nl, kdim + pad, n), BF16),
        compiler_params=_params("parallel", "parallel"),
        name="cast_pad_rows",
    )(w)


def _swiglu_kernel(x_ref, wg_ref, wu_ref, o_ref):
    x = x_ref[...]
    o_ref[...] = (_silu(_dot(x, wg_ref[...])) * _dot(x, wu_ref[...])).astype(o_ref.dtype)


def _swiglu_up(x, wg, wu, layer):
    m, kdim = x.shape
    n = wg.shape[-1]
    bm = _pick(m, 1040, 16)
    bn = _pick(n, 512, LANES)
    wspec = pl.BlockSpec((None, kdim, bn), lambda i, j: (layer, 0, j))
    return pl.pallas_call(
        _swiglu_kernel,
        grid=(m // bm, n // bn),
        in_specs=[pl.BlockSpec((bm, kdim), lambda i, j: (i, 0)), wspec, wspec],
        out_specs=pl.BlockSpec((bm, bn), lambda i, j: (i, j)),
        out_shape=jax.ShapeDtypeStruct((m, n), BF16),
        compiler_params=_params("parallel", "parallel"),
        name="swiglu_up",
    )(x, wg, wu)


def _rope_pair(pe, cos, sin):
    return pe * cos + pltpu.roll(pe, LANES // 2, 1) * sin


def _qprep_kernel(cq_ref, g_ref, wq_ref, wk_ref, cos_ref, sin_ref, o_ref, *, nheads, nope, dlat, scale):
    cqn = _rms_rows(cq_ref[...], g_ref[...]).astype(BF16)
    q = _dot(cqn, wq_ref[...])
    cos, sin = cos_ref[...], sin_ref[...]
    pe0 = nheads * nope
    for h in range(nheads):
        qn = q[:, h * nope:(h + 1) * nope].astype(BF16)
        o_ref[h, :, 0:dlat] = (_dot(qn, wk_ref[h]) * scale).astype(o_ref.dtype)
        pe = q[:, pe0 + h * LANES:pe0 + (h + 1) * LANES]
        o_ref[h, :, dlat:dlat + LANES] = (_rope_pair(pe, cos, sin) * scale).astype(o_ref.dtype)


def _qprep(cq, g_q, wq, wk, cos, sin, scale):
    m, cdim = cq.shape
    nheads, nope, dlat = wk.shape
    tm = _pick(m, 256, 16)
    full = lambda a: pl.BlockSpec(a.shape, lambda i: (0,) * a.ndim)
    row = lambda a: pl.BlockSpec((tm, a.shape[1]), lambda i: (i, 0))
    return pl.pallas_call(
        functools.partial(_qprep_kernel, nheads=nheads, nope=nope, dlat=dlat, scale=scale),
        grid=(m // tm,),
        in_specs=[row(cq), full(g_q), full(wq), full(wk), row(cos), row(sin)],
        out_specs=pl.BlockSpec((nheads, tm, dlat + LANES), lambda i: (0, i, 0)),
        out_shape=jax.ShapeDtypeStruct((nheads, m, dlat + LANES), BF16),
        compiler_params=_params("parallel"),
        name="mla_qprep",
    )(cq, g_q, wq, wk, cos, sin)


def _kvprep_kernel(u_ref, g_ref, cos_ref, sin_ref, row_ref, k_ref, *, dlat, rope):
    u = u_ref[...]
    lat = _rms_rows(u[:, :dlat], g_ref[...])
    r = _rope_pair(u[:, dlat:dlat + LANES], cos_ref[...], sin_ref[...])
    row_ref[:, 0:dlat] = lat
    row_ref[:, dlat:dlat + rope] = r[:, :rope]
    k_ref[:, 0:dlat] = lat.astype(k_ref.dtype)
    k_ref[:, dlat:dlat + LANES] = r.astype(k_ref.dtype)


def _kvprep(u, g_kv, cos, sin, rope):
    m = u.shape[0]
    dlat = g_kv.shape[1]
    tm = _pick(m, 512, 16)
    row = lambda w: pl.BlockSpec((tm, w), lambda i: (i, 0))
    return pl.pallas_call(
        functools.partial(_kvprep_kernel, dlat=dlat, rope=rope),
        grid=(m // tm,),
        in_specs=[row(u.shape[1]), pl.BlockSpec((1, dlat), lambda i: (0, 0)), row(LANES), row(LANES)],
        out_specs=[row(dlat + rope), row(dlat + LANES)],
        out_shape=[jax.ShapeDtypeStruct((m, dlat + rope), F32), jax.ShapeDtypeStruct((m, dlat + LANES), BF16)],
        compiler_params=_params("parallel"),
        name="mla_kvprep",
    )(u, g_kv, cos, sin)


def _online_softmax_step(m_ref, l_ref, acc_ref, s, pv, axis=1):
    m_prev = m_ref[...]
    m_new = jnp.maximum(m_prev, jnp.max(s, axis=axis, keepdims=True))
    alpha = jnp.exp(m_prev - m_new)
    p = jnp.exp(s - m_new)
    l_ref[...] = alpha * l_ref[...] + jnp.sum(p, axis=axis, keepdims=True)
    acc_ref[...] = alpha * acc_ref[...] + pv(p.astype(BF16))
    m_ref[...] = m_new


def _causal_blocks(q_start, tq, tk, step):
    n_full = q_start // tk
    n_all = (q_start + tq + tk - 1) // tk

    def run(masked):
        def body(kb, carry):
            step(kb, masked)
            return carry
        return body

    lax.fori_loop(0, n_full, run(False), 0)
    lax.fori_loop(n_full, n_all, run(True), 0)


def _query_tokens(q_start, tq, cols):
    return q_start + lax.rem(lax.broadcasted_iota(jnp.int32, (1, cols), 1), tq)


def _mask_keys(st, off, tok, masked):
    if not masked:
        return st
    key = off + lax.broadcasted_iota(jnp.int32, (st.shape[0], 1), 0)
    return jnp.where(key <= tok, st, NEG)


def _key_blocks_t(k, tk):
    rows, d = k.shape
    return jnp.transpose(k.reshape(rows // tk, tk, d), (0, 2, 1))


def _flash_mla_kernel(q_ref, k_ref, kt_ref, wv_ref, o_ref, m_ref, l_ref, acc_ref, *, tq, tk, nheads, dlat, dv):
    cols = nheads * tq
    q = q_ref[...].reshape(cols, q_ref.shape[-1])
    q_start = pl.program_id(1) * tq
    tok = _query_tokens(q_start, tq, cols)
    m_ref[...] = jnp.full_like(m_ref, NEG)
    l_ref[...] = jnp.zeros_like(l_ref)
    acc_ref[...] = jnp.zeros_like(acc_ref)

    def step(kb, masked):
        off = pl.multiple_of(kb * tk, tk)
        st = _mask_keys(_dot_nt(k_ref[pl.ds(off, tk), :], q), off, tok, masked)
        vt = kt_ref[kb, 0:dlat, :]
        _online_softmax_step(m_ref, l_ref, acc_ref, st, lambda p: _dot(vt, p), axis=0)

    _causal_blocks(q_start, tq, tk, step)
    ot = acc_ref[...] / l_ref[...]
    for h in range(nheads):
        oth = ot[:, h * tq:(h + 1) * tq].astype(BF16)
        o_ref[:, h * dv:(h + 1) * dv] = _dot_tn(oth, wv_ref[h]).astype(o_ref.dtype)


def _flash_mla(q, k, wv, batch, seq):
    nheads, _, dq = q.shape
    _, dlat, dv = wv.shape
    tq = _pick(seq, FLASH_TQ, LANES)
    tk = _pick(seq, FLASH_TK, LANES)
    nq, nkb = seq // tq, seq // tk
    cols = nheads * tq
    kt = _key_blocks_t(k[:batch * seq], tk)
    return pl.pallas_call(
        functools.partial(_flash_mla_kernel, tq=tq, tk=tk, nheads=nheads, dlat=dlat, dv=dv),
        grid=(batch, nq),
        in_specs=[pl.BlockSpec((nheads, tq, dq), lambda b, i: (0, b * nq + i, 0)),
                  pl.BlockSpec((seq, dq), lambda b, i: (b, 0)),
                  pl.BlockSpec((nkb, dq, tk), lambda b, i: (b, 0, 0)),
                  pl.BlockSpec(wv.shape, lambda b, i: (0, 0, 0))],
        out_specs=pl.BlockSpec((tq, nheads * dv), lambda b, i: (b * nq + i, 0)),
        out_shape=jax.ShapeDtypeStruct((batch * seq, nheads * dv), BF16),
        scratch_shapes=[pltpu.VMEM((1, cols), F32), pltpu.VMEM((1, cols), F32), pltpu.VMEM((dlat, cols), F32)],
        compiler_params=_params("parallel", "parallel"),
        name="flash_mla",
    )(q, k, kt, wv)


def _diff_lambda(lam_ref, lam_init):
    lp = lam_ref[...]
    s01 = jnp.sum(lp[0:1] * lp[1:2], axis=1, keepdims=True)
    s23 = jnp.sum(lp[2:3] * lp[3:4], axis=1, keepdims=True)
    return jnp.exp(s01) - jnp.exp(s23) + lam_init


def _diff_combine(o1, o2, lam, g, lam_init):
    return _rms_rows(o1 - lam * o2, g) * (1.0 - lam_init)


def _flash_diff_kernel(q_ref, k_ref, vt_ref, lam_ref, g_ref, o_ref, m_ref, l_ref, acc_ref, *, tq, tk, nh, dd, lam_init):
    cols = nh * tq
    q = q_ref[...]
    qs = [jnp.concatenate([q[:, (2 * h + j) * dd:(2 * h + j + 1) * dd] for h in range(nh)], axis=0) for j in range(2)]
    q_start = pl.program_id(1) * tq
    tok = _query_tokens(q_start, tq, cols)
    m_ref[...] = jnp.full_like(m_ref, NEG)
    l_ref[...] = jnp.zeros_like(l_ref)
    acc_ref[...] = jnp.zeros_like(acc_ref)

    def step(kb, masked):
        off = pl.multiple_of(kb * tk, tk)
        k = k_ref[pl.ds(off, tk), :].astype(BF16)
        vt = vt_ref[kb]
        for j in range(2):
            st = _mask_keys(_dot_nt(k[:, j * dd:(j + 1) * dd], qs[j]), off, tok, masked)
            _online_softmax_step(m_ref.at[j], l_ref.at[j], acc_ref.at[j], st, lambda p: _dot(vt, p), axis=0)

    _causal_blocks(q_start, tq, tk, step)
    lam = _diff_lambda(lam_ref, lam_init)
    ot = acc_ref[0] / l_ref[0] - lam * (acc_ref[1] / l_ref[1])
    rt = ot * lax.rsqrt(jnp.mean(ot * ot, axis=0, keepdims=True) + NORM_EPS) * g_ref[...] * (1.0 - lam_init)
    for h in range(nh):
        o_ref[:, h * 2 * dd:(h + 1) * 2 * dd] = rt[:, h * tq:(h + 1) * tq].T.astype(o_ref.dtype)


def _flash_diff(q, k, v, lam_p, g_sub, batch, seq, lam_init):
    dd = k.shape[1] // 2
    nh = q.shape[1] // (2 * dd)
    tq = _pick(seq, FLASH_TQ, LANES)
    tk = _pick(seq, FLASH_TK, LANES)
    nq, nkb = seq // tq, seq // tk
    cols = nh * tq
    vt = _key_blocks_t(v[:batch * seq].astype(BF16), tk)
    g_col = g_sub.reshape(2 * dd, 1)
    return pl.pallas_call(
        functools.partial(_flash_diff_kernel, tq=tq, tk=tk, nh=nh, dd=dd, lam_init=lam_init),
        grid=(batch, nq),
        in_specs=[pl.BlockSpec((tq, q.shape[1]), lambda b, i: (b * nq + i, 0)),
                  pl.BlockSpec((seq, 2 * dd), lambda b, i: (b, 0)),
                  pl.BlockSpec((nkb, 2 * dd, tk), lambda b, i: (b, 0, 0)),
                  pl.BlockSpec(lam_p.shape, lambda b, i: (0, 0)),
                  pl.BlockSpec(g_col.shape, lambda b, i: (0, 0))],
        out_specs=pl.BlockSpec((tq, nh * 2 * dd), lambda b, i: (b * nq + i, 0)),
        out_shape=jax.ShapeDtypeStruct((batch * seq, nh * 2 * dd), BF16),
        scratch_shapes=[pltpu.VMEM((2, 1, cols), F32), pltpu.VMEM((2, 1, cols), F32), pltpu.VMEM((2, 2 * dd, cols), F32)],
        compiler_params=_params("parallel", "parallel"),
        name="flash_diff",
    )(q, k, vt, lam_p, g_col)


def _single_key_update(m_ref, l_ref, acc_ref, s_new, v_new):
    m_prev = m_ref[...]
    m_new = jnp.maximum(m_prev, s_new)
    alpha = jnp.exp(m_prev - m_new)
    p = jnp.exp(s_new - m_new)
    l = alpha * l_ref[...] + p
    acc = alpha * acc_ref[...] + p.astype(BF16).astype(F32) * v_new
    return acc / l


def _decode_attn_kernel(pt_ref, q_ref, dq_ref, kn_ref, dkn_ref, dvn_ref, lam_ref, g_ref, *rest,
                        pps, page, nh_d, dd, dlat, row, lam_init):
    del pt_ref
    mla_pages, dk_pages, dv_pages = rest[:pps], rest[pps:2 * pps], rest[2 * pps:3 * pps]
    olat_ref, od_ref = rest[3 * pps:3 * pps + 2]
    kbuf, m1, l1, a1, m2, l2, a2 = rest[3 * pps + 2:]
    c = pl.program_id(1)

    @pl.when(c == 0)
    def _():
        kbuf[...] = jnp.zeros_like(kbuf)
        for m_ref, l_ref, a_ref in ((m1, l1, a1), (m2, l2, a2)):
            m_ref[...] = jnp.full_like(m_ref, NEG)
            l_ref[...] = jnp.zeros_like(l_ref)
            a_ref[...] = jnp.zeros_like(a_ref)

    for r in range(pps):
        kbuf[0:row, r * page:(r + 1) * page] = mla_pages[r][...].astype(BF16)
    kt = kbuf[...]
    q = q_ref[...]
    _online_softmax_step(m1, l1, a1, _dot(q, kt), lambda p: _dot_nt(p, kt[0:dlat]))

    k1 = jnp.concatenate([p[pl.ds(0, page, stride=2), :] for p in dk_pages], axis=0).astype(BF16)
    k2 = jnp.concatenate([p[pl.ds(1, page, stride=2), :] for p in dk_pages], axis=0).astype(BF16)
    dv = jnp.concatenate([p[...] for p in dv_pages], axis=0).astype(BF16)
    dq = dq_ref[...]
    first = lax.broadcasted_iota(jnp.int32, (2 * nh_d, 1), 0) < nh_d
    s2 = jnp.where(first, _dot_nt(dq, k1), _dot_nt(dq, k2))
    _online_softmax_step(m2, l2, a2, s2, lambda p: _dot(p, dv))

    @pl.when(c == pl.num_programs(1) - 1)
    def _():
        kn = kn_ref[...].astype(F32)
        sn = jnp.sum(q.astype(F32) * kn, axis=1, keepdims=True)
        olat_ref[...] = _single_key_update(m1, l1, a1, sn, kn[:, :dlat])
        dkn = dkn_ref[...].astype(BF16).astype(F32)
        dvn = dvn_ref[...].astype(BF16).astype(F32)
        dqf = dq.astype(F32)
        sn2 = jnp.where(first, jnp.sum(dqf * dkn[:, :dd], axis=1, keepdims=True),
                        jnp.sum(dqf * dkn[:, dd:], axis=1, keepdims=True))
        on = _single_key_update(m2, l2, a2, sn2, dvn)
        lam = _diff_lambda(lam_ref, lam_init)
        od_ref[...] = _diff_combine(on[:nh_d], on[nh_d:], lam, g_ref[...], lam_init).astype(od_ref.dtype)


def _decode_attn(page_table, q, dq, k_new, dk_new, dv_new, lam_p, g_sub, cache_mla, cache_dk, cache_dv, lam_init):
    nb, n_pages = page_table.shape
    n_pool, page, row = cache_mla.shape
    nheads, dq_w = q.shape[1:]
    dd = cache_dv.shape[-1] // 2
    nh_d = dq.shape[1] // 2
    dlat = dq_w - LANES
    pps = _pick(n_pages, DECODE_PAGES, 1)
    cache_mla = jnp.transpose(cache_mla, (0, 2, 1))
    cache_dk = cache_dk.reshape(n_pool, 2 * page, dd)

    def page_spec(rows, width, r):
        return pl.BlockSpec((None, rows, width), lambda b, c, pt: (pt[b * n_pages + c * pps + r], 0, 0))

    per_seq = lambda a: pl.BlockSpec((None,) + a.shape[1:], lambda b, c, pt: (b, 0, 0))
    full = lambda a: pl.BlockSpec(a.shape, lambda b, c, pt: (0, 0))
    grid_spec = pltpu.PrefetchScalarGridSpec(
        num_scalar_prefetch=1,
        grid=(nb, n_pages // pps),
        in_specs=([per_seq(q), per_seq(dq), per_seq(k_new), per_seq(dk_new), per_seq(dv_new), full(lam_p), full(g_sub)]
                  + [page_spec(row, page, r) for r in range(pps)]
                  + [page_spec(2 * page, dd, r) for r in range(pps)]
                  + [page_spec(page, 2 * dd, r) for r in range(pps)]),
        out_specs=[pl.BlockSpec((None, nheads, dlat), lambda b, c, pt: (b, 0, 0)),
                   pl.BlockSpec((None, nh_d, 2 * dd), lambda b, c, pt: (b, 0, 0))],
        scratch_shapes=[pltpu.VMEM((dq_w, pps * page), BF16),
                        pltpu.VMEM((nheads, 1), F32), pltpu.VMEM((nheads, 1), F32), pltpu.VMEM((nheads, dlat), F32),
                        pltpu.VMEM((2 * nh_d, 1), F32), pltpu.VMEM((2 * nh_d, 1), F32), pltpu.VMEM((2 * nh_d, 2 * dd), F32)],
    )
    return pl.pallas_call(
        functools.partial(_decode_attn_kernel, pps=pps, page=page, nh_d=nh_d, dd=dd, dlat=dlat, row=row, lam_init=lam_init),
        grid_spec=grid_spec,
        out_shape=[jax.ShapeDtypeStruct((nb, nheads, dlat), F32), jax.ShapeDtypeStruct((nb, nh_d, 2 * dd), BF16)],
        compiler_params=_params("parallel", "arbitrary"),
        name="decode_attn",
    )(page_table.reshape(-1), q, dq, k_new, dk_new, dv_new, lam_p, g_sub,
      *([cache_mla] * pps), *([cache_dk] * pps), *([cache_dv] * pps))


def _head_proj_kernel(x_ref, w_ref, o_ref):
    o_ref[...] = _dot(x_ref[...].astype(BF16), w_ref[...]).astype(o_ref.dtype)


def _head_proj(x, w):
    nheads, n, c = x.shape
    v = w.shape[2]
    return pl.pallas_call(
        _head_proj_kernel,
        grid=(nheads,),
        in_specs=[pl.BlockSpec((None, n, c), lambda h: (h, 0, 0)), pl.BlockSpec((None, c, v), lambda h: (h, 0, 0))],
        out_specs=pl.BlockSpec((n, v), lambda h: (0, h)),
        out_shape=jax.ShapeDtypeStruct((n, nheads * v), BF16),
        compiler_params=_params("parallel"),
        name="head_proj",
    )(x, w)


def _softplus(x):
    return jnp.maximum(x, 0.0) + jnp.log1p(jnp.exp(-jnp.abs(x)))


def _dtprep_kernel(raw_ref, bias_ref, alog_ref, dt_ref, da_ref, cum_ref):
    dt = _softplus(raw_ref[...] + bias_ref[...])
    da = dt * -jnp.exp(alog_ref[...])
    dt_ref[...] = dt
    da_ref[...] = da
    rows = da.shape[0]
    row = lax.broadcasted_iota(jnp.int32, da.shape, 0)
    cum = da
    shift = 1
    while shift < rows:
        cum = cum + jnp.where(row >= shift, pltpu.roll(cum, shift, 0), 0.0)
        shift *= 2
    cum_ref[...] = cum


def _dtprep(raw, bias, a_log, chunk):
    m, nh = raw.shape
    spec = pl.BlockSpec((chunk, nh), lambda i: (i, 0))
    vec = pl.BlockSpec((1, nh), lambda i: (0, 0))
    shape = jax.ShapeDtypeStruct((m, nh), F32)
    return pl.pallas_call(
        _dtprep_kernel,
        grid=(m // chunk,),
        in_specs=[spec, vec, vec],
        out_specs=[spec, spec, spec],
        out_shape=[shape, shape, shape],
        compiler_params=_params("parallel"),
        name="ssd_dtprep",
    )(raw, bias.reshape(1, nh), a_log.reshape(1, nh))


def _conv_kernel(x_ref, halo_ref, w_ref, b_ref, o_ref, buf, *, tt, width, halo):
    j = pl.program_id(1)
    buf[0:halo, :] = jnp.where(j == 0, 0.0, halo_ref[...])
    buf[halo:halo + tt, :] = x_ref[...]
    acc = b_ref[...] + buf[pl.ds(halo - (width - 1), tt), :] * w_ref[0:1, :]
    for k in range(1, width):
        acc = acc + buf[pl.ds(halo - (width - 1) + k, tt), :] * w_ref[k:k + 1, :]
    o_ref[...] = _silu(acc)


def _conv_prompt(x, w, b, batch, seq):
    cdim = x.shape[1]
    width = w.shape[0]
    halo = 8
    tt = _pick(seq, 256, 8)
    tc = _pick(cdim, 2048, LANES)
    nt = seq // tt
    return pl.pallas_call(
        functools.partial(_conv_kernel, tt=tt, width=width, halo=halo),
        grid=(batch, nt, cdim // tc),
        in_specs=[pl.BlockSpec((tt, tc), lambda b_, j, c: (b_ * nt + j, c)),
                  pl.BlockSpec((halo, tc), lambda b_, j, c: (jnp.maximum((b_ * nt + j) * (tt // halo) - 1, 0), c)),
                  pl.BlockSpec((width, tc), lambda b_, j, c: (0, c)),
                  pl.BlockSpec((1, tc), lambda b_, j, c: (0, c))],
        out_specs=pl.BlockSpec((tt, tc), lambda b_, j, c: (b_ * nt + j, c)),
        out_shape=jax.ShapeDtypeStruct((batch * seq, cdim), F32),
        scratch_shapes=[pltpu.VMEM((halo + tt, tc), F32)],
        compiler_params=_params("parallel", "parallel", "parallel"),
        name="ssd_conv",
    )(x, x, w, b.reshape(1, cdim))


def _conv_decode_kernel(s_ref, x_ref, w_ref, b_ref, o_ref, *, width):
    acc = b_ref[...] + s_ref[0] * w_ref[0:1, :]
    for k in range(1, width - 1):
        acc = acc + s_ref[k] * w_ref[k:k + 1, :]
    o_ref[...] = _silu(acc + x_ref[...] * w_ref[width - 1:width, :])


def _conv_decode(state_t, x, w, b):
    n, cdim = x.shape
    width = w.shape[0]
    tc = _pick(cdim, 2048, LANES)
    return pl.pallas_call(
        functools.partial(_conv_decode_kernel, width=width),
        grid=(cdim // tc,),
        in_specs=[pl.BlockSpec((width - 1, n, tc), lambda c: (0, 0, c)),
                  pl.BlockSpec((n, tc), lambda c: (0, c)),
                  pl.BlockSpec((width, tc), lambda c: (0, c)),
                  pl.BlockSpec((1, tc), lambda c: (0, c))],
        out_specs=pl.BlockSpec((n, tc), lambda c: (0, c)),
        out_shape=jax.ShapeDtypeStruct((n, cdim), F32),
        compiler_params=_params("parallel"),
        name="ssd_conv_decode",
    )(state_t, x, w, b.reshape(1, cdim))


def _gate_norm(y, z, g):
    return _rms_rows(y * _silu(z), g)


def _ssd_chunk_kernel(x_ref, b_ref, c_ref, z_ref, cumt_ref, dtt_ref, dskip_ref, g_ref,
                      y_ref, hout_ref, h_ref, y_scr, *, chunk, hpg, hd):
    ci = pl.program_id(2)

    @pl.when(ci == 0)
    def _():
        h_ref[...] = jnp.zeros_like(h_ref)

    gw = hpg * hd
    x = x_ref[...]
    bmat = b_ref[...].astype(BF16)
    cmat = c_ref[...].astype(BF16)
    cb = _dot_nt(cmat, bmat)
    h_all = h_ref[...]
    y_inter = _dot_nt(cmat, h_all.astype(BF16))
    cumt = cumt_ref[...]
    dtt = dtt_ref[...]

    srow = lax.broadcasted_iota(jnp.int32, (3 * hpg, hpg * LANES), 0) % hpg
    scol = lax.broadcasted_iota(jnp.int32, (3 * hpg, hpg * LANES), 1) // LANES
    cumb = _dot_tn(_split3(cumt), jnp.where(srow == scol, 1.0, 0.0).astype(BF16))

    causal = (lax.broadcasted_iota(jnp.int32, (chunk, chunk), 0) >= lax.broadcasted_iota(jnp.int32, (chunk, chunk), 1))
    even_head = (lax.broadcasted_iota(jnp.int32, (1, gw), 1) // hd) % 2 == 0
    x_even = jnp.where(even_head, x, 0.0).astype(BF16)
    x_odd = jnp.where(even_head, 0.0, x).astype(BF16)
    first_half = lax.broadcasted_iota(jnp.int32, (1, LANES), 1) < hd
    for p in range(hpg // 2):
        blk = slice(p * LANES, (p + 1) * LANES)
        y_pair = None
        for r, xm in ((2 * p, x_even), (2 * p + 1, x_odd)):
            cum_i = cumb[:, r * LANES:(r + 1) * LANES]
            seg = jnp.concatenate([cum_i - cumt[r:r + 1, k * LANES:(k + 1) * LANES] for k in range(chunk // LANES)], axis=1)
            wgt = cb * jnp.exp(jnp.where(causal, seg, -jnp.inf)) * dtt[r:r + 1, :]
            part = _dot(wgt.astype(BF16), xm[:, blk])
            y_pair = part if y_pair is None else y_pair + part
        cum_pair = jnp.where(first_half, cumb[:, 2 * p * LANES:(2 * p + 1) * LANES], cumb[:, (2 * p + 1) * LANES:(2 * p + 2) * LANES])
        y_scr[:, blk] = y_pair + y_inter[:, blk] * jnp.exp(cum_pair)

    xt = x.T
    scaled, decay = [], []
    for r in range(hpg):
        last = cumt[r:r + 1, chunk - 1:chunk]
        to_end = jnp.exp(last - cumt[r:r + 1, :]) * dtt[r:r + 1, :]
        scaled.append((xt[r * hd:(r + 1) * hd, :] * to_end).astype(BF16))
        decay.append(jnp.broadcast_to(jnp.exp(last), (hd, h_all.shape[1])))
    h_ref[...] = h_all * jnp.concatenate(decay, axis=0) + _dot(jnp.concatenate(scaled, axis=0), bmat)

    y = y_scr[...] + dskip_ref[...] * x
    y_ref[...] = _gate_norm(y, z_ref[...], g_ref[...]).astype(y_ref.dtype)

    @pl.when(ci == pl.num_programs(2) - 1)
    def _():
        hout_ref[...] = h_ref[...]


def _ssd_prompt(xbc, z, cum_t, dt_t, dskip, g_norm, batch, seq, chunk, ngroups, nstate):
    d_inner = z.shape[1]
    gw = d_inner // ngroups
    hpg = cum_t.shape[0] // ngroups
    hd = gw // hpg
    assert 2 * hd == LANES and hpg % 2 == 0 and chunk % LANES == 0
    nc = seq // chunk
    xoff = d_inner // nstate
    rows = lambda b, g, c: b * nc + c
    return pl.pallas_call(
        functools.partial(_ssd_chunk_kernel, chunk=chunk, hpg=hpg, hd=hd),
        grid=(batch, ngroups, nc),
        in_specs=[pl.BlockSpec((chunk, gw), lambda b, g, c: (rows(b, g, c), g)),
                  pl.BlockSpec((chunk, nstate), lambda b, g, c: (rows(b, g, c), xoff + g)),
                  pl.BlockSpec((chunk, nstate), lambda b, g, c: (rows(b, g, c), xoff + ngroups + g)),
                  pl.BlockSpec((chunk, gw), lambda b, g, c: (rows(b, g, c), g)),
                  pl.BlockSpec((hpg, chunk), lambda b, g, c: (g, rows(b, g, c))),
                  pl.BlockSpec((hpg, chunk), lambda b, g, c: (g, rows(b, g, c))),
                  pl.BlockSpec((1, gw), lambda b, g, c: (0, g)),
                  pl.BlockSpec((1, gw), lambda b, g, c: (0, g))],
        out_specs=[pl.BlockSpec((chunk, gw), lambda b, g, c: (rows(b, g, c), g)),
                   pl.BlockSpec((None, gw, nstate), lambda b, g, c: (b * ngroups + g, 0, 0))],
        out_shape=[jax.ShapeDtypeStruct((batch * seq, d_inner), BF16),
                   jax.ShapeDtypeStruct((batch * ngroups, gw, nstate), F32)],
        scratch_shapes=[pltpu.VMEM((gw, nstate), F32), pltpu.VMEM((chunk, gw), F32)],
        compiler_params=_params("parallel", "parallel", "arbitrary"),
        name="ssd_chunk_scan",
    )(xbc, xbc, xbc, z, cum_t, dt_t, dskip, g_norm)


def _split3(a):
    hi = a.astype(BF16)
    r1 = a - hi.astype(F32)
    mid = r1.astype(BF16)
    lo = (r1 - mid.astype(F32)).astype(BF16)
    return jnp.concatenate([hi, mid, lo], axis=0)


def _ssd_step_kernel(h_ref, x_ref, b_ref, c_ref, z_ref, dt_ref, da_ref, dskip_ref, g_ref, hout_ref, y_ref, y_scr,
                     *, sb, nstate):
    gw = x_ref.shape[1]
    x = x_ref[...]
    xdt = x * dt_ref[...]
    decay = jnp.exp(da_ref[...])
    prow = lax.broadcasted_iota(jnp.int32, (16, gw), 0)
    srow = lax.broadcasted_iota(jnp.int32, (48, 2 * nstate), 0) % 16
    scol = lax.broadcasted_iota(jnp.int32, (48, 2 * nstate), 1)
    sel = jnp.where(((srow == 0) & (scol < nstate)) | ((srow == 1) & (scol >= nstate)), 1.0, 0.0).astype(BF16)
    for s in range(sb):
        pair = jnp.where(prow == 0, xdt[s:s + 1, :], jnp.where(prow == 1, decay[s:s + 1, :], 0.0))
        bc = _dot_tn(_split3(pair), sel)
        hn = h_ref[s].reshape(gw, nstate) * bc[:, nstate:] + bc[:, :nstate] * b_ref[s:s + 1, :]
        hout_ref[s] = hn.reshape(hout_ref.shape[1:])
        crow = jnp.broadcast_to(c_ref[s:s + 1, :], (16, nstate)).astype(BF16)
        y_scr[s:s + 1, :] = _dot_nt(crow, hn.astype(BF16))[0:1]
    y = y_scr[...] + dskip_ref[...] * x
    y_ref[...] = _gate_norm(y, z_ref[...], g_ref[...]).astype(y_ref.dtype)


def _ssd_step(h, act, z, z_row0, dt_rep, da_rep, dskip, g_norm, ngroups):
    n, nheads, hd, nstate = h.shape
    d_inner = nheads * hd
    hpg = nheads // ngroups
    gw = hpg * hd
    sb = _pick(n, 8, 8)
    assert z_row0 % sb == 0
    xoff = d_inner // nstate
    hspec = pl.BlockSpec((sb, hpg, hd, nstate), lambda i, g: (i, g, 0, 0))
    chan = pl.BlockSpec((sb, gw), lambda i, g: (i, g))
    vec = pl.BlockSpec((1, gw), lambda i, g: (0, g))
    return pl.pallas_call(
        functools.partial(_ssd_step_kernel, sb=sb, nstate=nstate),
        grid=(n // sb, ngroups),
        in_specs=[hspec, chan,
                  pl.BlockSpec((sb, nstate), lambda i, g: (i, xoff + g)),
                  pl.BlockSpec((sb, nstate), lambda i, g: (i, xoff + ngroups + g)),
                  pl.BlockSpec((sb, gw), lambda i, g: (z_row0 // sb + i, g)),
                  chan, chan, vec, vec],
        out_specs=[hspec, chan],
        out_shape=[jax.ShapeDtypeStruct(h.shape, F32), jax.ShapeDtypeStruct((n, d_inner), F32)],
        scratch_shapes=[pltpu.VMEM((sb, gw), F32)],
        compiler_params=_params("parallel", "parallel"),
        name="ssd_step",
    )(h, act, act, act, z, dt_rep, da_rep, dskip, g_norm)


def _rot_cols(w):
    half = w.shape[-1] // 2
    return jnp.concatenate([-w[..., half:], w[..., :half]], axis=-1)


def _attn_layer(h, layer, mp, seq, batch, page_table, caches, cos, sin, w_in, g_q, g_kv, w_qb, w_kvb, lam_p, g_sub, w_out):
    cache_mla, cache_dk, cache_dv = caches
    q_lora = g_q.shape[0]
    dlat = g_kv.shape[0]
    rope = cache_mla.shape[-1] - dlat
    dd = cache_dv.shape[-1] // 2
    nheads = w_qb.shape[1]
    nope = w_qb.shape[2] - rope
    dv = w_kvb.shape[2] - nope
    assert 2 * rope == LANES, "the rope rotation is done inside one 128-lane chunk"
    lam_init = 0.8 - 0.6 * math.exp(-0.3 * layer)
    nd = h.shape[0] - mp

    o_ckv, o_kpe, o_dq = q_lora, q_lora + dlat, q_lora + dlat + rope
    o_dk = w_in.shape[1] - 4 * dd
    w_cq = w_in[:, :q_lora].astype(BF16)
    w_kpe = w_in[:, o_kpe:o_dq]
    w_ckr = jnp.concatenate([w_in[:, o_ckv:o_kpe], w_kpe, _rot_cols(w_kpe)], axis=1).astype(BF16)
    w_dq = w_in[:, o_dq:o_dk].astype(BF16)
    w_dk = w_in[:, o_dk:o_dk + 2 * dd].astype(BF16)
    w_dv = w_in[:, o_dk + 2 * dd:].astype(BF16)
    w_pe = w_qb[..., nope:]
    wq = jnp.concatenate([w_qb[..., :nope].reshape(q_lora, nheads * nope),
                          jnp.concatenate([w_pe, _rot_cols(w_pe)], axis=-1).reshape(q_lora, nheads * LANES)],
                         axis=1).astype(BF16)
    wk = jnp.transpose(w_kvb[..., :nope], (1, 2, 0)).astype(BF16)
    wv = jnp.transpose(w_kvb[..., nope:], (1, 0, 2)).astype(BF16)

    cq = _matmul(h, w_cq)
    ckr = _matmul(h, w_ckr)
    dq = _matmul(h, w_dq, out_dtype=BF16, scale=dd ** -0.5)
    dk = _matmul(h, w_dk)
    dv_rows = _matmul(h, w_dv)

    q = _qprep(cq, g_q.reshape(1, -1), wq, wk, cos, sin, (nope + rope) ** -0.5)
    mla_row, k_pad = _kvprep(ckr, g_kv.reshape(1, -1), cos, sin, rope)

    g_sub2 = g_sub.reshape(1, -1)
    o_mla_p = _flash_mla(q, k_pad, wv, batch, seq)
    o_d_p = _flash_diff(dq, dk, dv_rows, lam_p, g_sub2, batch, seq, lam_init)

    q_s = jnp.transpose(q[:, mp:], (1, 0, 2))
    nh_d = dq.shape[1] // (2 * dd)
    dq_s = jnp.transpose(dq[mp:].reshape(nd, nh_d, 2, dd), (0, 2, 1, 3)).reshape(nd, 2 * nh_d, dd)
    o_lat_s, o_d_s = _decode_attn(page_table, q_s, dq_s, k_pad[mp:, None], dk[mp:, None], dv_rows[mp:, None],
                                  lam_p, g_sub2, cache_mla, cache_dk, cache_dv, lam_init)
    o_mla_s = _head_proj(jnp.transpose(o_lat_s, (1, 0, 2)), wv)

    o = jnp.concatenate([jnp.concatenate([o_mla_p, o_d_p], axis=1),
                         jnp.concatenate([o_mla_s, o_d_s.reshape(nd, -1)], axis=1)], axis=0)
    out = _matmul(o, w_out, layer=layer // 2)
    new = (mla_row[:mp].reshape(batch, seq, -1), dk[:mp].reshape(batch, seq, 2, dd), dv_rows[:mp].reshape(batch, seq, -1),
           mla_row[mp:].reshape(nd, 1, -1), dk[mp:].reshape(nd, 1, 2, dd), dv_rows[mp:].reshape(nd, 1, -1))
    return out, new


def _ssd_layer(h, j, mp, seq, batch, conv_state, ssm_state, w_in, conv_w, conv_b, dt_bias, a_log, d_skip, g_norm, w_out):
    nd, nheads, hd, nstate = ssm_state.shape
    d_inner = nheads * hd
    cdim = conv_w.shape[1]
    ngroups = (cdim - d_inner) // (2 * nstate)
    hpg = nheads // ngroups
    chunk = _pick(seq, 256, 8)
    width = conv_w.shape[0]

    z = _matmul(h, w_in, layer=j, cols=(0, d_inner))
    xbc = _matmul(h, w_in, layer=j, cols=(d_inner, cdim))
    dt_raw = _matmul(h, w_in, layer=j, cols=(d_inner + cdim, nheads))

    dskip = jnp.repeat(d_skip, hd).reshape(1, d_inner)
    g2 = g_norm.reshape(1, d_inner)

    dt_p, _, cum_p = _dtprep(dt_raw[:mp], dt_bias, a_log, chunk)
    xbc_act = _conv_prompt(xbc, conv_w, conv_b, batch, seq)
    y_p, h_p = _ssd_prompt(xbc_act, z, cum_p.T, dt_p.T, dskip, g2, batch, seq, chunk, ngroups, nstate)
    conv_p = jnp.stack([xbc[b * seq + seq - (width - 1):(b + 1) * seq] for b in range(batch)])
    ssm_p = h_p.reshape(batch, nheads, hd, nstate)

    xbc_s = xbc[mp:]
    dt_s, da_s, _ = _dtprep(dt_raw[mp:], dt_bias, a_log, nd)
    act_s = _conv_decode(jnp.transpose(conv_state, (1, 0, 2)), xbc_s, conv_w, conv_b)
    ssm_s, y_s = _ssd_step(ssm_state, act_s, z, mp, jnp.repeat(dt_s, hd, axis=1), jnp.repeat(da_s, hd, axis=1),
                           dskip, g2, ngroups)
    conv_s = jnp.concatenate([conv_state[:, 1:], xbc_s[:, None]], axis=1)

    out = _matmul(jnp.concatenate([y_p, y_s.astype(BF16)], axis=0), w_out, layer=j)
    return out, (conv_p, ssm_p, conv_s, ssm_s)


def kernel(x_prompt, x_sample, cache_mla_l0, cache_dk_l0, cache_dv_l0, state_conv_l1, state_ssm_l1, cache_mla_l2, cache_dk_l2, cache_dv_l2, state_conv_l3, state_ssm_l3, page_table, norm_mix_pre, norm_mix_post, norm_ffn_pre, norm_ffn_post, ffn_w_gate, ffn_w_up, ffn_w_down, attn_w_in, mla_g_q, mla_g_kv, mla_w_qb, mla_w_kvb, diff_lambda, diff_g_sub, attn_w_out, ssm_w_in, ssm_conv_w, ssm_conv_b, ssm_dt_bias, ssm_a_log, ssm_d, ssm_g_norm, ssm_w_out):
    batch, seq, d_model = x_prompt.shape
    nd, dec_seq, _ = x_sample.shape
    assert dec_seq == 1, "decode rows are one new token per sequence"
    mp = batch * seq
    depth = norm_mix_pre.shape[0]
    attn_caches = {0: (cache_mla_l0, cache_dk_l0, cache_dv_l0), 2: (cache_mla_l2, cache_dk_l2, cache_dv_l2)}
    ssm_states = {1: (state_conv_l1, state_ssm_l1), 3: (state_conv_l3, state_ssm_l3)}

    rope = cache_mla_l0.shape[-1] - mla_g_kv.shape[1]
    half = rope // 2
    past_len = page_table.shape[1] * cache_mla_l0.shape[1]
    pos = jnp.concatenate([jnp.tile(jnp.arange(seq), batch), jnp.full((nd,), past_len)])
    inv = jnp.exp(-math.log(ROPE_BASE) * jnp.arange(half, dtype=F32) / half)
    ang = pos.astype(F32)[:, None] * inv[None, :]
    pad = jnp.zeros((mp + nd, LANES - rope), F32)
    cos = jnp.concatenate([jnp.cos(ang), jnp.cos(ang), pad], axis=1)
    sin = jnp.concatenate([jnp.sin(ang), jnp.sin(ang), pad], axis=1)

    d_ff = ffn_w_gate.shape[2]
    ff_pad = -d_ff % FFN_ALIGN
    wg = _cast_pad_cols(ffn_w_gate, ff_pad)
    wu = _cast_pad_cols(ffn_w_up, ff_pad)
    wd = _cast_pad_rows(ffn_w_down, ff_pad)
    w_attn_out = attn_w_out.astype(BF16)
    w_ssm_in = ssm_w_in.astype(BF16)
    w_ssm_out = ssm_w_out.astype(BF16)

    x = (x_prompt.reshape(mp, d_model), x_sample.reshape(nd, d_model))
    h = _rms(x, norm_mix_pre[0])
    new = {}
    for layer in range(depth):
        j = layer // 2
        if layer % 2 == 0:
            o, new[layer] = _attn_layer(h, layer, mp, seq, batch, page_table, attn_caches[layer], cos, sin,
                                        attn_w_in[j], mla_g_q[j], mla_g_kv[j], mla_w_qb[j], mla_w_kvb[j],
                                        diff_lambda[j], diff_g_sub[j], w_attn_out)
        else:
            o, new[layer] = _ssd_layer(h, j, mp, seq, batch, *ssm_states[layer], w_ssm_in, ssm_conv_w[j], ssm_conv_b[j],
                                       ssm_dt_bias[j], ssm_a_log[j], ssm_d[j], ssm_g_norm[j], w_ssm_out)
        x, h = _add_rms(x, o, norm_mix_post[layer], norm_ffn_pre[layer])
        o = _matmul(_swiglu_up(h, wg, wu, layer), wd, layer=layer, bk=2816)
        last = layer + 1 == depth
        x, h = _add_rms(x, o, norm_ffn_post[layer], None if last else norm_mix_pre[layer + 1],
                        split_rows=mp if last else None)
    outs = [x[0].reshape(batch, seq, d_model), x[1].reshape(nd, 1, d_model)]
    for layer in range(depth):
        outs.extend(new[layer])
    return tuple(outs)
```
